```python
import math
import jax, jax.numpy as jnp
from jax import lax
import numpy as np

D_MODEL = 2048
BATCH = 4
SEQ = 2048
DEPTH = 2

PLE_DIM = 256
HEAD_DIM = 64
GROUP_W = D_MODEL // 4
FOX_HEADS = GROUP_W // HEAD_DIM
DIFF_HEADS = GROUP_W // (2 * HEAD_DIM)
POOL_GROUPS = 4
POOL_CH = GROUP_W // POOL_GROUPS
POOL_WINDOWS = (2, 4, 8, 16)
SWA_HEADS = GROUP_W // HEAD_DIM
SWA_KV_HEADS = 2
SWA_GROUP = SWA_HEADS // SWA_KV_HEADS
SWA_WINDOW = 128
Q_BLOCK = 128
MIX_W = 4 * GROUP_W
IN_SIZES = (GROUP_W, GROUP_W, GROUP_W, FOX_HEADS,
            GROUP_W, GROUP_W, GROUP_W,
            GROUP_W,
            SWA_HEADS * HEAD_DIM, SWA_KV_HEADS * HEAD_DIM, SWA_KV_HEADS * HEAD_DIM)
IN_W = sum(IN_SIZES)
REL_BUCKETS = 32
REL_MAX_DIST = 128
REL_HEADS = DIFF_HEADS + SWA_HEADS
MOE_GROUPS = 4
EXPERTS_PER_GROUP = 4
N_EXPERTS = MOE_GROUPS * EXPERTS_PER_GROUP
TOP_K_IN_GROUP = 2
D_EXPERT = D_MODEL // 4
ALPHA = (2 * DEPTH) ** 0.25
BETA = (8 * DEPTH) ** -0.25
LN_EPS = 1e-5

kernel_name = "hymba_style_fox_diff_pool_swa_hmoe"


def _layer_norm(x, g, b):
    xf = x.astype(jnp.float32)
    mu = xf.mean(-1, keepdims=True)
    var = jnp.square(xf - mu).mean(-1, keepdims=True)
    return ((xf - mu) * lax.rsqrt(var + LN_EPS) * g + b).astype(x.dtype)


def _rms_norm(x, g):
    xf = x.astype(jnp.float32)
    return (xf * lax.rsqrt(jnp.mean(xf * xf, -1, keepdims=True) + LN_EPS) * g).astype(x.dtype)


def _t5_bucket(n):
    max_exact = REL_BUCKETS // 2
    nf = jnp.maximum(n, 1).astype(jnp.float32)
    large = max_exact + (jnp.log(nf / max_exact) / math.log(REL_MAX_DIST / max_exact)
                         * (REL_BUCKETS - max_exact)).astype(jnp.int32)
    large = jnp.minimum(large, REL_BUCKETS - 1)
    return jnp.where(n < max_exact, n, large)


def _split_cols(proj):
    bounds, acc = [], 0
    for s in IN_SIZES[:-1]:
        acc += s
        bounds.append(acc)
    return jnp.split(proj, bounds, axis=-1)


def _sweep(block_fn, seq):
    out = lax.map(block_fn, jnp.arange(seq // Q_BLOCK) * Q_BLOCK)
    nb, b, qb, h, dv = out.shape
    return jnp.moveaxis(out, 0, 1).reshape(b, nb * qb, h, dv)


def _forgetting_attention(q, k, v, f_logit):
    _, S_, _, d = q.shape
    c = jnp.cumsum(jax.nn.log_sigmoid(f_logit.astype(jnp.float32)), axis=1)
    c_t = jnp.transpose(c, (0, 2, 1))
    kpos = jnp.arange(S_)
    scale = d ** -0.5

    def block(start):
        qb = lax.dynamic_slice_in_dim(q, start, Q_BLOCK, axis=1)
        cq = lax.dynamic_slice_in_dim(c_t, start, Q_BLOCK, axis=2)
        logits = jnp.einsum('bqhd,bkhd->bhqk', qb, k, preferred_element_type=jnp.float32) * scale
        logits = logits + cq[..., :, None] - c_t[..., None, :]
        qpos = start + jnp.arange(Q_BLOCK)
        mask = kpos[None, :] <= qpos[:, None]
        probs = jax.nn.softmax(jnp.where(mask, logits, -jnp.inf), axis=-1)
        return jnp.einsum('bhqk,bkhd->bqhd', probs.astype(v.dtype), v)

    return _sweep(block, S_)


def _diff_attention(q, k, v, rel_bias, lam, lam_init, sub_g):
    _, S_, _, _, d = q.shape
    kpos = jnp.arange(S_)
    scale = d ** -0.5

    def block(start):
        qb = lax.dynamic_slice_in_dim(q, start, Q_BLOCK, axis=1)
        logits = jnp.einsum('bqhmd,bkhmd->bhmqk', qb, k, preferred_element_type=jnp.float32) * scale
        dist = (start + jnp.arange(Q_BLOCK))[:, None] - kpos[None, :]
        bias = jnp.transpose(rel_bias[jnp.maximum(dist, 0)], (2, 0, 1))
        logits = logits + bias[None, :, None].astype(jnp.float32)
        probs = jax.nn.softmax(jnp.where(dist >= 0, logits, -jnp.inf), axis=-1)
        attn = probs[:, :, 0] - lam * probs[:, :, 1]
        return jnp.einsum('bhqk,bkhe->bqhe', attn.astype(v.dtype), v)

    o = _sweep(block, S_)
    o = _rms_norm(o, sub_g) * (1.0 - lam_init)
    return o.reshape(o.shape[0], S_, -1)


def _pool_mixer(u, pool_w, pool_scale):
    B_, S_, C = u.shape
    uf = u.astype(jnp.float32)
    cs = jnp.pad(jnp.cumsum(uf, axis=1), ((0, 0), (1, 0), (0, 0)))
    win = jnp.repeat(jnp.array(POOL_WINDOWS, dtype=jnp.int32), POOL_CH)
    t1 = jnp.arange(1, S_ + 1, dtype=jnp.int32)[:, None]
    lo = jnp.maximum(t1 - win[None, :], 0)
    lower = jnp.take_along_axis(cs, jnp.broadcast_to(lo[None], (B_, S_, C)), axis=1)
    cnt = jnp.minimum(t1, win[None, :]).astype(jnp.float32)
    pooled = ((cs[:, 1:] - lower) / cnt - uf).astype(u.dtype).reshape(B_, S_, POOL_GROUPS, POOL_CH)
    y = jnp.einsum('bsgc,gcd->bsgd', pooled, pool_w).reshape(B_, S_, C)
    return y * pool_scale


def _sliding_window_attention(q, k, v, rel_bias, sinks):
    B_, S_, Hq, d = q.shape
    nb = S_ // Q_BLOCK
    qb = q.reshape(B_, nb, Q_BLOCK, SWA_KV_HEADS, SWA_GROUP, d)

    def windows(t):
        tp = jnp.pad(t, ((0, 0), (Q_BLOCK, 0), (0, 0), (0, 0))).reshape(B_, nb + 1, Q_BLOCK, SWA_KV_HEADS, d)
        return jnp.concatenate([tp[:, :-1], tp[:, 1:]], axis=2)

    kw, vw = windows(k), windows(v)
    logits = jnp.einsum('bnqhgd,bnkhd->bnhgqk', qb, kw, preferred_element_type=jnp.float32) * d ** -0.5
    rel = jnp.arange(Q_BLOCK)[:, None] + Q_BLOCK - jnp.arange(2 * Q_BLOCK)[None, :]
    in_win = (rel >= 0) & (rel < SWA_WINDOW)
    kvalid = (jnp.arange(nb)[:, None] * Q_BLOCK - Q_BLOCK + jnp.arange(2 * Q_BLOCK)[None, :]) >= 0
    mask = in_win[None] & kvalid[:, None, :]
    bias = jnp.transpose(rel_bias[jnp.clip(rel, 0, SWA_WINDOW - 1)], (2, 0, 1))
    bias = bias.reshape(SWA_KV_HEADS, SWA_GROUP, Q_BLOCK, 2 * Q_BLOCK).astype(jnp.float32)
    logits = jnp.where(mask[None, :, None, None], logits + bias[None, None], -jnp.inf)
    sink = sinks.astype(jnp.float32).reshape(SWA_KV_HEADS, SWA_GROUP)[None, None, :, :, None, None]
    m = jnp.maximum(logits.max(-1, keepdims=True), sink)
    e = jnp.exp(logits - m)
    probs = e / (e.sum(-1, keepdims=True) + jnp.exp(sink - m))
    out = jnp.einsum('bnhgqk,bnkhd->bnqhgd', probs.astype(v.dtype), vw)
    return out.reshape(B_, S_, Hq * d)


def _hier_moe(h, rg_w, rg_b, re_w, re_b, w_gate, w_up, w_down):
    B_, S_, D = h.shape
    t = h.reshape(-1, D)
    g_probs = jax.nn.softmax((t @ rg_w).astype(jnp.float32) + rg_b, axis=-1)
    g_sel = jnp.argmax(g_probs, axis=-1)
    g_p = jnp.take_along_axis(g_probs, g_sel[:, None], axis=-1)
    e_logits = ((t @ re_w).astype(jnp.float32) + re_b).reshape(-1, MOE_GROUPS, EXPERTS_PER_GROUP)
    e_in = jnp.take_along_axis(e_logits, g_sel[:, None, None], axis=1)[:, 0]
    top_v, top_i = lax.top_k(e_in, TOP_K_IN_GROUP)
    w = jax.nn.softmax(top_v, axis=-1) * g_p
    ids = g_sel[:, None] * EXPERTS_PER_GROUP + top_i
    gates = jnp.sum(jax.nn.one_hot(ids, N_EXPERTS, dtype=jnp.float32) * w[..., None], axis=1)
    a = jnp.einsum('td,edf->tef', t, w_gate)
    u = jnp.einsum('td,edf->tef', t, w_up)
    hid = jax.nn.silu(a) * u * gates[..., None].astype(t.dtype)
    y = jnp.einsum('tef,efd->td', hid, w_down)
    return y.reshape(B_, S_, D)


def setup_inputs(seed: int = 0) -> dict:
    key = jax.random.key(seed)
    ks = jax.random.split(key, 27)
    n = lambda k, s: jax.random.normal(k, s, dtype=jnp.float32)
    L, D = DEPTH, D_MODEL
    return {
        "x": n(ks[0], (BATCH, SEQ, D)),
        "p": n(ks[1], (L, BATCH, SEQ, PLE_DIM)),
        "rel_table": 0.3 * n(ks[2], (REL_BUCKETS, REL_HEADS)),
        "w_in": n(ks[3], (L, D, IN_W)) * D ** -0.5,
        "b_f": 0.1 * n(ks[4], (L, FOX_HEADS)),
        "lam_q1": 0.1 * n(ks[5], (L, HEAD_DIM)),
        "lam_k1": 0.1 * n(ks[6], (L, HEAD_DIM)),
        "lam_q2": 0.1 * n(ks[7], (L, HEAD_DIM)),
        "lam_k2": 0.1 * n(ks[8], (L, HEAD_DIM)),
        "diff_norm_g": 1.0 + 0.02 * n(ks[9], (L, 2 * HEAD_DIM)),
        "pool_w": n(ks[10], (L, POOL_GROUPS, POOL_CH, POOL_CH)) * POOL_CH ** -0.5,
        "pool_scale": 1.0 + 0.02 * n(ks[11], (L, GROUP_W)),
        "sinks": 0.5 * n(ks[12], (L, SWA_HEADS)),
        "w_o": n(ks[13], (L, MIX_W, D)) * MIX_W ** -0.5 * BETA,
        "ln1_g": 1.0 + 0.02 * n(ks[14], (L, D)),
        "ln1_b": 0.02 * n(ks[15], (L, D)),
        "router_g_w": n(ks[16], (L, D, MOE_GROUPS)) * D ** -0.5,
        "router_g_b": 0.01 * n(ks[17], (L, MOE_GROUPS)),
        "router_e_w": n(ks[18], (L, D, N_EXPERTS)) * D ** -0.5,
        "router_e_b": 0.01 * n(ks[19], (L, N_EXPERTS)),
        "w_gate": n(ks[20], (L, N_EXPERTS, D, D_EXPERT)) * D ** -0.5,
        "w_up": n(ks[21], (L, N_EXPERTS, D, D_EXPERT)) * D ** -0.5,
        "w_down": n(ks[22], (L, N_EXPERTS, D_EXPERT, D)) * D_EXPERT ** -0.5 * BETA,
        "ple_gate_w": n(ks[23], (L, D, D)) * D ** -0.5,
        "ple_up_w": n(ks[24], (L, PLE_DIM, D)) * PLE_DIM ** -0.5 * BETA,
        "ln2_g": 1.0 + 0.02 * n(ks[25], (L, D)),
        "ln2_b": 0.02 * n(ks[26], (L, D)),
    }


def reference(x, p, rel_table, w_in, b_f, lam_q1, lam_k1, lam_q2, lam_k2, diff_norm_g,
              pool_w, pool_scale, sinks, w_o, ln1_g, ln1_b, router_g_w, router_g_b,
              router_e_w, router_e_b, w_gate, w_up, w_down, ple_gate_w, ple_up_w,
              ln2_g, ln2_b):
    B_, S_, _ = x.shape
    bias_by_dist = rel_table[_t5_bucket(jnp.arange(S_, dtype=jnp.int32))]
    diff_bias = bias_by_dist[:, :DIFF_HEADS]
    swa_bias = bias_by_dist[:SWA_WINDOW, DIFF_HEADS:]
    for i in range(DEPTH):
        proj = x @ w_in[i]
        fq, fk, fv, ff, dq, dk, dv, pu, sq, sk, sv = _split_cols(proj)
        y_fox = _forgetting_attention(
            fq.reshape(B_, S_, FOX_HEADS, HEAD_DIM), fk.reshape(B_, S_, FOX_HEADS, HEAD_DIM),
            fv.reshape(B_, S_, FOX_HEADS, HEAD_DIM), ff + b_f[i]).reshape(B_, S_, GROUP_W)
        lam_init = 0.8 - 0.6 * math.exp(-0.3 * i)
        lam = (jnp.exp(jnp.sum(lam_q1[i].astype(jnp.float32) * lam_k1[i].astype(jnp.float32)))
               - jnp.exp(jnp.sum(lam_q2[i].astype(jnp.float32) * lam_k2[i].astype(jnp.float32)))
               + lam_init)
        y_diff = _diff_attention(
            dq.reshape(B_, S_, DIFF_HEADS, 2, HEAD_DIM), dk.reshape(B_, S_, DIFF_HEADS, 2, HEAD_DIM),
            dv.reshape(B_, S_, DIFF_HEADS, 2 * HEAD_DIM), diff_bias, lam, lam_init, diff_norm_g[i])
        y_pool = _pool_mixer(pu, pool_w[i], pool_scale[i])
        y_swa = _sliding_window_attention(
            sq.reshape(B_, S_, SWA_HEADS, HEAD_DIM), sk.reshape(B_, S_, SWA_KV_HEADS, HEAD_DIM),
            sv.reshape(B_, S_, SWA_KV_HEADS, HEAD_DIM), swa_bias, sinks[i])
        mix = jnp.concatenate([y_fox, y_diff, y_pool, y_swa], axis=-1)
        h = _layer_norm(ALPHA * x + mix @ w_o[i], ln1_g[i], ln1_b[i])
        y_moe = _hier_moe(h, router_g_w[i], router_g_b[i], router_e_w[i], router_e_b[i],
                          w_gate[i], w_up[i], w_down[i])
        ple = jax.nn.sigmoid(h @ ple_gate_w[i]) * (p[i] @ ple_up_w[i])
        x = _layer_norm(ALPHA * h + y_moe + ple, ln2_g[i], ln2_b[i])
    return x
```

```python
import functools
import math

import jax
import jax.numpy as jnp
import numpy as np
from jax import lax
from jax.experimental import pallas as pl
from jax.experimental.pallas import tpu as pltpu

F32 = jnp.float32
BF16 = jnp.bfloat16

D_MODEL = 2048
HEAD_DIM = 64
GROUP_W = D_MODEL // 4
FOX_HEADS = GROUP_W // HEAD_DIM
DIFF_HEADS = GROUP_W // (2 * HEAD_DIM)
POOL_GROUPS = 4
POOL_CH = GROUP_W // POOL_GROUPS
POOL_WINDOWS = (2, 4, 8, 16)
SWA_HEADS = GROUP_W // HEAD_DIM
SWA_KV_HEADS = 2
SWA_WINDOW = 128
REL_BUCKETS = 32
REL_MAX_DIST = 128
MOE_GROUPS = 4
EXPERTS_PER_GROUP = 4
N_EXPERTS = MOE_GROUPS * EXPERTS_PER_GROUP
D_EXPERT = D_MODEL // 4
PLE_DIM = 256
DEPTH = 2
ALPHA = (2 * DEPTH) ** 0.25
LN_EPS = 1e-5

LANES = 128
VMEM_LIMIT = 56 * 1024 * 1024

PROJ_W = 9 * GROUP_W
COL_FQ, COL_FK, COL_FV, COL_DQ, COL_DK, COL_DV, COL_PU, COL_SQ, COL_SKV = range(9)

ATT_TILE = 256
NEG = -1e30


def _cparams(sem, vmem=VMEM_LIMIT):
    return pltpu.CompilerParams(dimension_semantics=sem, vmem_limit_bytes=vmem)


def _inproj_kernel(x_ref, w_ref, wff_ref, o_ref, ff_ref, xb_ref):
    j = pl.program_id(1)

    @pl.when(j == 0)
    def _():
        xb_ref[...] = x_ref[...].astype(BF16)
        ff_ref[...] = jnp.dot(xb_ref[...], wff_ref[...], preferred_element_type=F32)

    o_ref[...] = jnp.dot(xb_ref[...], w_ref[...], preferred_element_type=F32).astype(BF16)


def _inproj(x2d, w_main, w_ff, tm=1024, tn=GROUP_W):
    t, d = x2d.shape
    n = w_main.shape[1]
    tm = min(tm, t)
    return pl.pallas_call(
        _inproj_kernel,
        grid=(t // tm, n // tn),
        in_specs=[
            pl.BlockSpec((tm, d), lambda i, j: (i, 0)),
            pl.BlockSpec((d, tn), lambda i, j: (0, j)),
            pl.BlockSpec((d, LANES), lambda i, j: (0, 0)),
        ],
        out_specs=[
            pl.BlockSpec((tm, tn), lambda i, j: (i, j)),
            pl.BlockSpec((tm, LANES), lambda i, j: (i, 0)),
        ],
        out_shape=[
            jax.ShapeDtypeStruct((t, n), BF16),
            jax.ShapeDtypeStruct((t, LANES), F32),
        ],
        scratch_shapes=[pltpu.VMEM((tm, d), BF16)],
        compiler_params=_cparams(("parallel", "arbitrary")),
        name="inproj",
    )(x2d, w_main, w_ff)


def _gate_kernel(ff_ref, bf_ref, ccol_ref, crow_ref):
    f = ff_ref[0] + bf_ref[...]
    ls = jnp.minimum(f, 0.0) - jnp.log(1.0 + jnp.exp(-jnp.abs(f)))
    c = ls.T
    s_len = c.shape[1]
    lane = lax.broadcasted_iota(jnp.int32, c.shape, 1)
    shift = 1
    while shift < s_len:
        c = c + jnp.where(lane >= shift, pltpu.roll(c, shift, axis=1), 0.0)
        shift *= 2
    crow_ref[0] = c[:FOX_HEADS]
    ccol_ref[0] = c.T


def _gate_cumsum(ff3, bf_pad):
    b, s, _ = ff3.shape
    return pl.pallas_call(
        _gate_kernel,
        grid=(b,),
        in_specs=[
            pl.BlockSpec((1, s, LANES), lambda i: (i, 0, 0)),
            pl.BlockSpec((1, LANES), lambda i: (0, 0)),
        ],
        out_specs=[
            pl.BlockSpec((1, s, LANES), lambda i: (i, 0, 0)),
            pl.BlockSpec((1, FOX_HEADS, s), lambda i: (i, 0, 0)),
        ],
        out_shape=[
            jax.ShapeDtypeStruct((b, s, LANES), F32),
            jax.ShapeDtypeStruct((b, FOX_HEADS, s), F32),
        ],
        compiler_params=_cparams(("parallel",)),
        name="gate_cumsum",
    )(ff3, bf_pad)


def _causal_pairs(n_tiles):
    qi, ki = [], []
    for q in range(n_tiles):
        for k in range(q + 1):
            qi.append(q)
            ki.append(k)
    return jnp.asarray(qi, jnp.int32), jnp.asarray(ki, jnp.int32)


def _half_mask(half):
    lane = lax.broadcasted_iota(jnp.int32, (1, LANES), 1)
    return (lane >= HEAD_DIM) if half else (lane < HEAD_DIM)


def _qk(qm, k2):
    return lax.dot_general(qm, k2, (((1,), (1,)), ((), ())), preferred_element_type=F32)


def _fox_kernel(qi_ref, ki_ref, q_ref, k_ref, v_ref, ccol_ref, crow_ref, mask_ref,
                o_ref, m_sc, l_sc, acc_sc):
    p_id = pl.program_id(1)
    qi = qi_ref[p_id]
    ki = ki_ref[p_id]
    scale = HEAD_DIM ** -0.5

    @pl.when(ki == 0)
    def _():
        m_sc[...] = jnp.full(m_sc.shape, NEG, F32)
        l_sc[...] = jnp.zeros(l_sc.shape, F32)
        acc_sc[...] = jnp.zeros(acc_sc.shape, F32)

    mask = mask_ref[(ki == qi).astype(jnp.int32)]
    low = _half_mask(0)
    for hp in range(FOX_HEADS // 2):
        cols = slice(hp * LANES, (hp + 1) * LANES)
        q2 = q_ref[0, :, cols]
        k2 = k_ref[0, :, cols]
        v2 = v_ref[0, :, cols]
        alphas, pvs = [], []
        for half in range(2):
            h = 2 * hp + half
            qm = jnp.where(_half_mask(half), q2, jnp.zeros_like(q2)) * scale
            s = _qk(qm, k2)
            s = s + ccol_ref[0, :, h:h + 1] - crow_ref[0, h:h + 1, :] + mask
            m_prev = m_sc[h]
            m_new = jnp.maximum(m_prev, jnp.max(s, axis=1, keepdims=True))
            alpha = jnp.exp(m_prev - m_new)
            p = jnp.exp(s - m_new)
            l_sc[h] = alpha * l_sc[h] + jnp.sum(p, axis=1, keepdims=True)
            m_sc[h] = m_new
            alphas.append(alpha)
            pvs.append(jnp.dot(p.astype(BF16), v2, preferred_element_type=F32))
        acc_sc[:, cols] = (jnp.where(low, alphas[0], alphas[1]) * acc_sc[:, cols]
                           + jnp.where(low, pvs[0], pvs[1]))

    @pl.when(ki == qi)
    def _():
        for hp in range(FOX_HEADS // 2):
            cols = slice(hp * LANES, (hp + 1) * LANES)
            l2 = jnp.where(low, l_sc[2 * hp], l_sc[2 * hp + 1])
            o_ref[0, :, cols] = (acc_sc[:, cols] / l2).astype(BF16)


def _causal_mask_tiles(t):
    r = lax.broadcasted_iota(jnp.int32, (t, t), 0)
    c = lax.broadcasted_iota(jnp.int32, (t, t), 1)
    return jnp.stack([jnp.zeros((t, t), F32), jnp.where(c <= r, 0.0, NEG).astype(F32)])


def _fox_attention(proj3, ccol, crow, t=ATT_TILE):
    b, s, _ = proj3.shape
    qi, ki = _causal_pairs(s // t)
    grid_spec = pltpu.PrefetchScalarGridSpec(
        num_scalar_prefetch=2,
        grid=(b, qi.shape[0]),
        in_specs=[
            pl.BlockSpec((1, t, GROUP_W), lambda i, p, qi, ki: (i, qi[p], COL_FQ)),
            pl.BlockSpec((1, t, GROUP_W), lambda i, p, qi, ki: (i, ki[p], COL_FK)),
            pl.BlockSpec((1, t, GROUP_W), lambda i, p, qi, ki: (i, ki[p], COL_FV)),
            pl.BlockSpec((1, t, LANES), lambda i, p, qi, ki: (i, qi[p], 0)),
            pl.BlockSpec((1, FOX_HEADS, t), lambda i, p, qi, ki: (i, 0, ki[p])),
            pl.BlockSpec((2, t, t), lambda i, p, qi, ki: (0, 0, 0)),
        ],
        out_specs=pl.BlockSpec((1, t, GROUP_W), lambda i, p, qi, ki: (i, qi[p], 0)),
        scratch_shapes=[
            pltpu.VMEM((FOX_HEADS, t, 1), F32),
            pltpu.VMEM((FOX_HEADS, t, 1), F32),
            pltpu.VMEM((t, GROUP_W), F32),
        ],
    )
    return pl.pallas_call(
        _fox_kernel,
        grid_spec=grid_spec,
        out_shape=jax.ShapeDtypeStruct((b, s, GROUP_W), BF16),
        compiler_params=_cparams(("parallel", "arbitrary")),
        name="fox_attention",
    )(qi, ki, proj3, proj3, proj3, ccol, crow, _causal_mask_tiles(t))


def _diff_kernel(qi_ref, ki_ref, q_ref, k_ref, v_ref, bias_ref, lam_ref, g_ref,
                 o_ref, m_sc, l_sc, acc_sc, *, lam_init):
    p_id = pl.program_id(1)
    qi = qi_ref[p_id]
    ki = ki_ref[p_id]
    scale = HEAD_DIM ** -0.5

    @pl.when(ki == 0)
    def _():
        m_sc[...] = jnp.full(m_sc.shape, NEG, F32)
        l_sc[...] = jnp.zeros(l_sc.shape, F32)
        acc_sc[...] = jnp.zeros(acc_sc.shape, F32)

    tile = jnp.minimum(qi - ki, 2)
    for h in range(DIFF_HEADS):
        cols = slice(h * LANES, (h + 1) * LANES)
        q2 = q_ref[0, :, cols]
        k2 = k_ref[0, :, cols]
        v2 = v_ref[0, :, cols]
        bias = bias_ref[h, tile]
        for mm in range(2):
            idx = 2 * h + mm
            qm = jnp.where(_half_mask(mm), q2, jnp.zeros_like(q2)) * scale
            s = _qk(qm, k2) + bias
            m_prev = m_sc[idx]
            m_new = jnp.maximum(m_prev, jnp.max(s, axis=1, keepdims=True))
            alpha = jnp.exp(m_prev - m_new)
            p = jnp.exp(s - m_new)
            l_sc[idx] = alpha * l_sc[idx] + jnp.sum(p, axis=1, keepdims=True)
            m_sc[idx] = m_new
            pv = jnp.dot(p.astype(BF16), v2, preferred_element_type=F32)
            acc_sc[mm, :, cols] = alpha * acc_sc[mm, :, cols] + pv

    @pl.when(ki == qi)
    def _():
        lv = lam_ref[...]
        lam = (jnp.exp(jnp.sum(lv[0:1] * lv[1:2], axis=1, keepdims=True))
               - jnp.exp(jnp.sum(lv[2:3] * lv[3:4], axis=1, keepdims=True)) + lam_init)
        for h in range(DIFF_HEADS):
            cols = slice(h * LANES, (h + 1) * LANES)
            o = acc_sc[0, :, cols] / l_sc[2 * h] - lam * (acc_sc[1, :, cols] / l_sc[2 * h + 1])
            ms = jnp.mean(o * o, axis=1, keepdims=True)
            o = o * lax.rsqrt(ms + LN_EPS) * g_ref[...] * (1.0 - lam_init)
            o_ref[0, :, cols] = o.astype(BF16)


def _diff_bias_tiles(diff_bias, t):
    r = lax.broadcasted_iota(jnp.int32, (t, t), 0)
    c = lax.broadcasted_iota(jnp.int32, (t, t), 1)
    bt = diff_bias.T.astype(F32)
    diag = jnp.where((c <= r)[None], bt[:, jnp.maximum(r - c, 0)], NEG)
    nxt = bt[:, t + r - c]
    far = jnp.broadcast_to(bt[:, -1][:, None, None], nxt.shape)
    return jnp.stack([diag, nxt, far], axis=1)


def _diff_attention(proj3, bias_tiles, lam_vecs, g_pad, lam_init, t=ATT_TILE):
    b, s, _ = proj3.shape
    qi, ki = _causal_pairs(s // t)
    grid_spec = pltpu.PrefetchScalarGridSpec(
        num_scalar_prefetch=2,
        grid=(b, qi.shape[0]),
        in_specs=[
            pl.BlockSpec((1, t, GROUP_W), lambda i, p, qi, ki: (i, qi[p], COL_DQ)),
            pl.BlockSpec((1, t, GROUP_W), lambda i, p, qi, ki: (i, ki[p], COL_DK)),
            pl.BlockSpec((1, t, GROUP_W), lambda i, p, qi, ki: (i, ki[p], COL_DV)),
            pl.BlockSpec((DIFF_HEADS, 3, t, t), lambda i, p, qi, ki: (0, 0, 0, 0)),
            pl.BlockSpec((4, LANES), lambda i, p, qi, ki: (0, 0)),
            pl.BlockSpec((1, LANES), lambda i, p, qi, ki: (0, 0)),
        ],
        out_specs=pl.BlockSpec((1, t, GROUP_W), lambda i, p, qi, ki: (i, qi[p], 0)),
        scratch_shapes=[
            pltpu.VMEM((2 * DIFF_HEADS, t, 1), F32),
            pltpu.VMEM((2 * DIFF_HEADS, t, 1), F32),
            pltpu.VMEM((2, t, GROUP_W), F32),
        ],
    )
    return pl.pallas_call(
        functools.partial(_diff_kernel, lam_init=lam_init),
        grid_spec=grid_spec,
        out_shape=jax.ShapeDtypeStruct((b, s, GROUP_W), BF16),
        compiler_params=_cparams(("parallel", "arbitrary")),
        name="diff_attention",
    )(qi, ki, proj3, proj3, proj3, bias_tiles, lam_vecs, g_pad)


def _swa_kernel(q_ref, kc_ref, kp_ref, vc_ref, vp_ref, bias_ref, sink_ref, o_ref):
    n = pl.program_id(1)
    scale = HEAD_DIM ** -0.5
    prev_pen = jnp.where(n > 0, 0.0, NEG).astype(F32)
    low = _half_mask(0)
    group = SWA_HEADS // SWA_KV_HEADS
    for hp in range(SWA_HEADS // 2):
        cols = slice(hp * LANES, (hp + 1) * LANES)
        q2 = q_ref[0, :, cols]
        outs = []
        for half in range(2):
            h = 2 * hp + half
            kv = slice((h // group) * LANES, (h // group + 1) * LANES)
            qm = jnp.where(_half_mask(half), q2, jnp.zeros_like(q2)) * scale
            s_c = _qk(qm, kc_ref[0, :, kv]) + bias_ref[h, 1]
            s_p = _qk(qm, kp_ref[0, :, kv]) + bias_ref[h, 0] + prev_pen
            sink = sink_ref[:, h:h + 1]
            m = jnp.maximum(jnp.maximum(jnp.max(s_c, axis=1, keepdims=True),
                                        jnp.max(s_p, axis=1, keepdims=True)), sink)
            e_c = jnp.exp(s_c - m)
            e_p = jnp.exp(s_p - m)
            den = (jnp.sum(e_c, axis=1, keepdims=True) + jnp.sum(e_p, axis=1, keepdims=True)
                   + jnp.exp(sink - m))
            pv = (jnp.dot(e_c.astype(BF16), vc_ref[0, :, kv], preferred_element_type=F32)
                  + jnp.dot(e_p.astype(BF16), vp_ref[0, :, kv], preferred_element_type=F32))
            outs.append(pv / den)
        o_ref[0, :, cols] = jnp.where(low, outs[0], outs[1]).astype(BF16)


def _swa_bias_tiles(swa_bias):
    w = SWA_WINDOW
    r = lax.broadcasted_iota(jnp.int32, (w, w), 0)
    c = lax.broadcasted_iota(jnp.int32, (w, w), 1)
    bt = swa_bias.T.astype(F32)
    rel_prev = r + w - c
    prev = jnp.where((rel_prev < w)[None], bt[:, jnp.minimum(rel_prev, w - 1)], NEG)
    rel_cur = r - c
    cur = jnp.where((rel_cur >= 0)[None], bt[:, jnp.maximum(rel_cur, 0)], NEG)
    return jnp.stack([prev, cur], axis=1)


def _swa_attention(proj3, bias_tiles, sink_pad):
    b, s, _ = proj3.shape
    w = SWA_WINDOW
    kv_w = SWA_KV_HEADS * LANES
    k_blk = COL_SKV * GROUP_W // kv_w
    prev = lambda i, n: (i, jnp.maximum(n - 1, 0), k_blk)
    prev_v = lambda i, n: (i, jnp.maximum(n - 1, 0), k_blk + 1)
    return pl.pallas_call(
        _swa_kernel,
        grid=(b, s // w),
        in_specs=[
            pl.BlockSpec((1, w, GROUP_W), lambda i, n: (i, n, COL_SQ)),
            pl.BlockSpec((1, w, kv_w), lambda i, n: (i, n, k_blk)),
            pl.BlockSpec((1, w, kv_w), prev),
            pl.BlockSpec((1, w, kv_w), lambda i, n: (i, n, k_blk + 1)),
            pl.BlockSpec((1, w, kv_w), prev_v),
            pl.BlockSpec((SWA_HEADS, 2, w, w), lambda i, n: (0, 0, 0, 0)),
            pl.BlockSpec((1, LANES), lambda i, n: (0, 0)),
        ],
        out_specs=pl.BlockSpec((1, w, GROUP_W), lambda i, n: (i, n, 0)),
        out_shape=jax.ShapeDtypeStruct((b, s, GROUP_W), BF16),
        compiler_params=_cparams(("parallel", "arbitrary")),
        name="swa_attention",
    )(proj3, proj3, proj3, proj3, proj3, bias_tiles, sink_pad)


def _pool_kernel(u_ref, w_ref, sc_ref, o_ref):
    s_len = u_ref.shape[1]
    row = lax.broadcasted_iota(jnp.int32, (s_len, POOL_CH), 0)
    for g in range(POOL_GROUPS):
        cols = slice(g * POOL_CH, (g + 1) * POOL_CH)
        u = u_ref[0, :, cols].astype(F32)
        win = u
        shift = 1
        while shift < POOL_WINDOWS[g]:
            win = win + jnp.where(row >= shift, pltpu.roll(win, shift, axis=0), 0.0)
            shift *= 2
        cnt = jnp.minimum(row + 1, POOL_WINDOWS[g]).astype(F32)
        pooled = (win / cnt - u).astype(BF16)
        y = jnp.dot(pooled, w_ref[g], preferred_element_type=F32) * sc_ref[:, cols]
        o_ref[0, :, cols] = y.astype(BF16)


def _pool_mixer(proj3, pool_w_bf, pool_scale_row):
    b, s, _ = proj3.shape
    return pl.pallas_call(
        _pool_kernel,
        grid=(b,),
        in_specs=[
            pl.BlockSpec((1, s, GROUP_W), lambda i: (i, 0, COL_PU)),
            pl.BlockSpec((POOL_GROUPS, POOL_CH, POOL_CH), lambda i: (0, 0, 0)),
            pl.BlockSpec((1, GROUP_W), lambda i: (0, 0)),
        ],
        out_specs=pl.BlockSpec((1, s, GROUP_W), lambda i: (i, 0, 0)),
        out_shape=jax.ShapeDtypeStruct((b, s, GROUP_W), BF16),
        compiler_params=_cparams(("parallel",)),
        name="pool_mixer",
    )(proj3, pool_w_bf, pool_scale_row)


def _layer_norm(z, g, b):
    mu = jnp.mean(z, axis=1, keepdims=True)
    zc = z - mu
    var = jnp.mean(zc * zc, axis=1, keepdims=True)
    return zc * lax.rsqrt(var + LN_EPS) * g + b


def _outproj_kernel(yf_ref, yd_ref, yp_ref, ys_ref, x_ref, wo_ref, g_ref, b_ref, wr_ref, rb_ref,
                    h32_ref, h16_ref, rl_ref):
    acc = jnp.dot(yf_ref[...], wo_ref[0], preferred_element_type=F32)
    acc += jnp.dot(yd_ref[...], wo_ref[1], preferred_element_type=F32)
    acc += jnp.dot(yp_ref[...], wo_ref[2], preferred_element_type=F32)
    acc += jnp.dot(ys_ref[...], wo_ref[3], preferred_element_type=F32)
    h = _layer_norm(ALPHA * x_ref[...] + acc, g_ref[...], b_ref[...])
    h32_ref[...] = h
    hb = h.astype(BF16)
    h16_ref[...] = hb
    rl_ref[...] = jnp.dot(hb, wr_ref[...], preferred_element_type=F32) + rb_ref[...]


def _outproj_ln(ys, x2d, wo4, g, b, wr, rb, tm=256):
    t, d = x2d.shape
    row = lambda i: (i, 0)
    const2 = lambda i: (0, 0)
    once = pl.Buffered(1)
    return pl.pallas_call(
        _outproj_kernel,
        grid=(t // tm,),
        in_specs=[pl.BlockSpec((tm, GROUP_W), row)] * 4 + [
            pl.BlockSpec((tm, d), row),
            pl.BlockSpec((4, GROUP_W, d), lambda i: (0, 0, 0), pipeline_mode=once),
            pl.BlockSpec((1, d), const2),
            pl.BlockSpec((1, d), const2),
            pl.BlockSpec((d, LANES), const2, pipeline_mode=once),
            pl.BlockSpec((1, LANES), const2),
        ],
        out_specs=[
            pl.BlockSpec((tm, d), row),
            pl.BlockSpec((tm, d), row),
            pl.BlockSpec((tm, LANES), row),
        ],
        out_shape=[
            jax.ShapeDtypeStruct((t, d), F32),
            jax.ShapeDtypeStruct((t, d), BF16),
            jax.ShapeDtypeStruct((t, LANES), F32),
        ],
        compiler_params=_cparams(("parallel",)),
        name="outproj_ln",
    )(*ys, x2d, wo4, g, b, wr, rb)


def _first_max(vals):
    best = vals[0]
    for v in vals[1:]:
        best = jnp.maximum(best, v)
    idx = jnp.full(best.shape, len(vals) - 1, jnp.int32)
    for j in range(len(vals) - 2, -1, -1):
        idx = jnp.where(vals[j] == best, j, idx)
    return best, idx


def _router_kernel(rl_ref, o_ref):
    lt = rl_ref[...].T
    gl = [lt[g:g + 1] for g in range(MOE_GROUPS)]
    gmax, gsel = _first_max(gl)
    gsum = gl[0] * 0.0
    for v in gl:
        gsum = gsum + jnp.exp(v - gmax)
    g_p = 1.0 / gsum
    e_in = []
    for j in range(EXPERTS_PER_GROUP):
        v = lt[MOE_GROUPS + j:MOE_GROUPS + j + 1]
        for g in range(1, MOE_GROUPS):
            row = MOE_GROUPS + g * EXPERTS_PER_GROUP + j
            v = jnp.where(gsel == g, lt[row:row + 1], v)
        e_in.append(v)
    v1, i1 = _first_max(e_in)
    rest = [jnp.where(i1 == j, NEG, e_in[j]) for j in range(EXPERTS_PER_GROUP)]
    v2, i2 = _first_max(rest)
    e21 = jnp.exp(v2 - v1)
    w1 = g_p / (1.0 + e21)
    w2 = g_p * e21 / (1.0 + e21)
    sub = lax.broadcasted_iota(jnp.int32, lt.shape, 0)
    out = jnp.zeros(lt.shape, F32)
    for e in range(N_EXPERTS):
        g, j = divmod(e, EXPERTS_PER_GROUP)
        gate = jnp.where(gsel == g, jnp.where(i1 == j, w1, jnp.where(i2 == j, w2, 0.0)), 0.0)
        out = jnp.where(sub == e, gate, out)
    o_ref[...] = out.T


def _router(rlog, tr=1024):
    t = rlog.shape[0]
    return pl.pallas_call(
        _router_kernel,
        grid=(t // tr,),
        in_specs=[pl.BlockSpec((tr, LANES), lambda i: (i, 0))],
        out_specs=pl.BlockSpec((tr, LANES), lambda i: (i, 0)),
        out_shape=jax.ShapeDtypeStruct((t, LANES), F32),
        compiler_params=_cparams(("parallel",)),
        name="router",
    )(rlog)


def _moe_kernel(h_ref, r_ref, wg_ref, wu_ref, wd_ref, y_ref, acc_sc):
    e = pl.program_id(1)

    @pl.when(e == 0)
    def _():
        acc_sc[...] = jnp.zeros(acc_sc.shape, F32)

    h = h_ref[...]
    a = jnp.dot(h, wg_ref[0], preferred_element_type=F32)
    u = jnp.dot(h, wu_ref[0], preferred_element_type=F32)
    lane = lax.broadcasted_iota(jnp.int32, (1, LANES), 1)
    gate = jnp.sum(jnp.where(lane == e, r_ref[...], 0.0), axis=1, keepdims=True)
    hid = (a / (1.0 + jnp.exp(-a))) * u * gate
    acc_sc[...] += jnp.dot(hid.astype(BF16), wd_ref[0], preferred_element_type=F32)

    @pl.when(e == pl.num_programs(1) - 1)
    def _():
        y_ref[...] = acc_sc[...]


def _moe(h16, route, wg, wu, wd, tm=512):
    t, d = h16.shape
    n_e, _, f = wg.shape
    return pl.pallas_call(
        _moe_kernel,
        grid=(t // tm, n_e),
        in_specs=[
            pl.BlockSpec((tm, d), lambda i, e: (i, 0)),
            pl.BlockSpec((tm, LANES), lambda i, e: (i, 0)),
            pl.BlockSpec((1, d, f), lambda i, e: (e, 0, 0)),
            pl.BlockSpec((1, d, f), lambda i, e: (e, 0, 0)),
            pl.BlockSpec((1, f, d), lambda i, e: (e, 0, 0)),
        ],
        out_specs=pl.BlockSpec((tm, d), lambda i, e: (i, 0)),
        out_shape=jax.ShapeDtypeStruct((t, d), F32),
        scratch_shapes=[pltpu.VMEM((tm, d), F32)],
        compiler_params=_cparams(("parallel", "arbitrary")),
        name="moe",
    )(h16, route, wg, wu, wd)


def _ple_kernel(h32_ref, h16_ref, y_ref, p_ref, wg_ref, wu_ref, g_ref, b_ref, o_ref):
    gate = jnp.dot(h16_ref[...], wg_ref[...], preferred_element_type=F32)
    up = jnp.dot(p_ref[...].astype(BF16), wu_ref[...], preferred_element_type=F32)
    ple = up / (1.0 + jnp.exp(-gate))
    o_ref[...] = _layer_norm(ALPHA * h32_ref[...] + y_ref[...] + ple, g_ref[...], b_ref[...])


def _ple_ln(h32, h16, y, p2d, wg, wu, g, b, tm=256):
    t, d = h32.shape
    row = lambda i: (i, 0)
    const2 = lambda i: (0, 0)
    once = pl.Buffered(1)
    return pl.pallas_call(
        _ple_kernel,
        grid=(t // tm,),
        in_specs=[
            pl.BlockSpec((tm, d), row),
            pl.BlockSpec((tm, d), row),
            pl.BlockSpec((tm, d), row),
            pl.BlockSpec((tm, PLE_DIM), row),
            pl.BlockSpec((d, d), const2, pipeline_mode=once),
            pl.BlockSpec((PLE_DIM, d), const2, pipeline_mode=once),
            pl.BlockSpec((1, d), const2),
            pl.BlockSpec((1, d), const2),
        ],
        out_specs=pl.BlockSpec((tm, d), row),
        out_shape=jax.ShapeDtypeStruct((t, d), F32),
        compiler_params=_cparams(("parallel",)),
        name="ple_ln",
    )(h32, h16, y, p2d, wg, wu, g, b)


def _t5_bucket(n):
    max_exact = REL_BUCKETS // 2
    nf = jnp.maximum(n, 1).astype(F32)
    large = max_exact + (jnp.log(nf / max_exact) / math.log(REL_MAX_DIST / max_exact)
                         * (REL_BUCKETS - max_exact)).astype(jnp.int32)
    large = jnp.minimum(large, REL_BUCKETS - 1)
    return jnp.where(n < max_exact, n, large)


def _pad_lanes(v, width=LANES):
    v = v.astype(F32).reshape(1, -1)
    return jnp.pad(v, ((0, 0), (0, width - v.shape[1])))


def _reorder_w_in(w):
    o_ff = 3 * GROUP_W
    o_d = o_ff + FOX_HEADS
    o_sk = o_d + 5 * GROUP_W
    o_sv = o_sk + SWA_KV_HEADS * HEAD_DIM
    sk = [w[:, o_sk + g * HEAD_DIM:o_sk + (g + 1) * HEAD_DIM] for g in range(SWA_KV_HEADS)]
    sv = [w[:, o_sv + g * HEAD_DIM:o_sv + (g + 1) * HEAD_DIM] for g in range(SWA_KV_HEADS)]
    main = jnp.concatenate(
        [w[:, :o_ff], w[:, o_d:o_sk]] + [m for g in sk for m in (g, g)] + [m for g in sv for m in (g, g)],
        axis=1)
    w_ff = jnp.pad(w[:, o_ff:o_d], ((0, 0), (0, LANES - FOX_HEADS)))
    return main.astype(BF16), w_ff.astype(BF16)


def kernel(x, p, rel_table, w_in, b_f, lam_q1, lam_k1, lam_q2, lam_k2, diff_norm_g, pool_w, pool_scale, sinks, w_o, ln1_g, ln1_b, router_g_w, router_g_b, router_e_w, router_e_b, w_gate, w_up, w_down, ple_gate_w, ple_up_w, ln2_g, ln2_b):
    b, s, d = x.shape
    t = b * s
    depth = w_in.shape[0]
    assert d == D_MODEL and s % (2 * ATT_TILE) == 0 and t % 1024 == 0

    bias_by_dist = rel_table[_t5_bucket(jnp.arange(s, dtype=jnp.int32))]
    diff_tiles = _diff_bias_tiles(bias_by_dist[:, :DIFF_HEADS], ATT_TILE)
    swa_tiles = _swa_bias_tiles(bias_by_dist[:SWA_WINDOW, DIFF_HEADS:])

    x2d = x.reshape(t, d)
    for i in range(depth):
        w_main, w_ff = _reorder_w_in(w_in[i])
        proj, ff = _inproj(x2d, w_main, w_ff)
        proj3 = proj.reshape(b, s, PROJ_W)

        ccol, crow = _gate_cumsum(ff.reshape(b, s, LANES), _pad_lanes(b_f[i]))
        y_fox = _fox_attention(proj3, ccol, crow)

        lam_init = 0.8 - 0.6 * math.exp(-0.3 * i)
        lam_vecs = jnp.concatenate(
            [_pad_lanes(v) for v in (lam_q1[i], lam_k1[i], lam_q2[i], lam_k2[i])], axis=0)
        y_diff = _diff_attention(proj3, diff_tiles, lam_vecs, _pad_lanes(diff_norm_g[i]), lam_init)

        y_pool = _pool_mixer(proj3, pool_w[i].astype(BF16), pool_scale[i].reshape(1, GROUP_W))
        y_swa = _swa_attention(proj3, swa_tiles, _pad_lanes(sinks[i]))

        w_r = jnp.concatenate([router_g_w[i], router_e_w[i]], axis=1)
        w_r = jnp.pad(w_r, ((0, 0), (0, LANES - w_r.shape[1]))).astype(BF16)
        r_b = _pad_lanes(jnp.concatenate([router_g_b[i], router_e_b[i]]))
        ys = [y.reshape(t, GROUP_W) for y in (y_fox, y_diff, y_pool, y_swa)]
        h32, h16, rlog = _outproj_ln(
            ys, x2d, w_o[i].astype(BF16).reshape(4, GROUP_W, d),
            ln1_g[i].reshape(1, d), ln1_b[i].reshape(1, d), w_r, r_b)

        route = _router(rlog)
        y_moe = _moe(h16, route, w_gate[i].astype(BF16), w_up[i].astype(BF16),
                     w_down[i].astype(BF16))
        x2d = _ple_ln(h32, h16, y_moe, p[i].reshape(t, PLE_DIM), ple_gate_w[i].astype(BF16),
                      ple_up_w[i].astype(BF16), ln2_g[i].reshape(1, d), ln2_b[i].reshape(1, d))
    return x2d.reshape(b, s, d)
```

```python
import functools
import math

import jax
import jax.numpy as jnp
from jax import lax
from jax.experimental import pallas as pl
from jax.experimental.pallas import tpu as pltpu

F32 = jnp.float32
BF16 = jnp.bfloat16

D_MODEL = 2048
HEAD_DIM = 64
GROUP_W = D_MODEL // 4
FOX_HEADS = GROUP_W // HEAD_DIM
DIFF_HEADS = GROUP_W // (2 * HEAD_DIM)
POOL_GROUPS = 4
POOL_CH = GROUP_W // POOL_GROUPS
POOL_WINDOWS = (2, 4, 8, 16)
SWA_HEADS = GROUP_W // HEAD_DIM
SWA_KV_HEADS = 2
SWA_GROUP = SWA_HEADS // SWA_KV_HEADS
SWA_WINDOW = 128
REL_BUCKETS = 32
REL_MAX_DIST = 128
MOE_GROUPS = 4
EXPERTS_PER_GROUP = 4
N_EXPERTS = MOE_GROUPS * EXPERTS_PER_GROUP
D_EXPERT = D_MODEL // 4
PLE_DIM = 256
DEPTH = 2
ALPHA = (2 * DEPTH) ** 0.25
LN_EPS = 1e-5

LANES = 128
VMEM_LIMIT = 56 * 1024 * 1024

PROJ_W = 9 * GROUP_W
COL_FQ, COL_FK, COL_FV, COL_DQ, COL_DK, COL_DV, COL_PU, COL_SQ, COL_SKV = range(9)

ATT_TILE = 256
NEG = -1e30

Q_C = HEAD_DIM
K_C = HEAD_DIM + 3


def _cparams(sem, vmem=VMEM_LIMIT):
    return pltpu.CompilerParams(dimension_semantics=sem, vmem_limit_bytes=vmem)


def _inproj_kernel(x_ref, w_ref, wff_ref, o_ref, ff_ref, xb_ref):
    j = pl.program_id(1)

    @pl.when(j == 0)
    def _():
        xb_ref[...] = x_ref[...].astype(BF16)
        ff_ref[...] = jnp.dot(xb_ref[...], wff_ref[...], preferred_element_type=F32)

    o_ref[...] = jnp.dot(xb_ref[...], w_ref[...], preferred_element_type=F32).astype(BF16)


def _inproj(x2d, w_main, w_ff, tm=1024, tn=GROUP_W):
    t, d = x2d.shape
    n = w_main.shape[1]
    tm = min(tm, t)
    return pl.pallas_call(
        _inproj_kernel,
        grid=(t // tm, n // tn),
        in_specs=[
            pl.BlockSpec((tm, d), lambda i, j: (i, 0)),
            pl.BlockSpec((d, tn), lambda i, j: (0, j)),
            pl.BlockSpec((d, LANES), lambda i, j: (0, 0)),
        ],
        out_specs=[
            pl.BlockSpec((tm, tn), lambda i, j: (i, j)),
            pl.BlockSpec((tm, LANES), lambda i, j: (i, 0)),
        ],
        out_shape=[
            jax.ShapeDtypeStruct((t, n), BF16),
            jax.ShapeDtypeStruct((t, LANES), F32),
        ],
        scratch_shapes=[pltpu.VMEM((tm, d), BF16)],
        compiler_params=_cparams(("parallel", "arbitrary")),
        name="inproj",
    )(x2d, w_main, w_ff)


def _gate_kernel(ff_ref, bf_ref, c_ref):
    f = ff_ref[0] + bf_ref[...]
    ls = jnp.minimum(f, 0.0) - jnp.log(1.0 + jnp.exp(-jnp.abs(f)))
    c = ls.T
    s_len = c.shape[1]
    lane = lax.broadcasted_iota(jnp.int32, c.shape, 1)
    shift = 1
    while shift < s_len:
        c = c + jnp.where(lane >= shift, pltpu.roll(c, shift, axis=1), 0.0)
        shift *= 2
    c_ref[0] = c.T


def _gate_cumsum(ff3, bf_pad):
    b, s, _ = ff3.shape
    return pl.pallas_call(
        _gate_kernel,
        grid=(b,),
        in_specs=[
            pl.BlockSpec((1, s, LANES), lambda i: (i, 0, 0)),
            pl.BlockSpec((1, LANES), lambda i: (0, 0)),
        ],
        out_specs=pl.BlockSpec((1, s, LANES), lambda i: (i, 0, 0)),
        out_shape=jax.ShapeDtypeStruct((b, s, LANES), F32),
        compiler_params=_cparams(("parallel",)),
        name="gate_cumsum",
    )(ff3, bf_pad)


def _lane_iota():
    return lax.broadcasted_iota(jnp.int32, (1, LANES), 1)


def _split3(c):
    hi = c.astype(BF16).astype(F32)
    r = c - hi
    lo = r.astype(BF16).astype(F32)
    lo2 = (r - lo).astype(BF16).astype(F32)
    return hi, lo, lo2


def _prep_kernel(fq_ref, fk_ref, fv_ref, sq_ref, skv_ref, c_ref,
                 fqa_ref, fka_ref, fva_ref, sqa_ref, ska_ref, sva_ref):
    scale = HEAD_DIM ** -0.5
    lane = _lane_iota()
    low = lane < HEAD_DIM
    ones_q = jnp.where((lane >= K_C) & (lane < K_C + 3), 1.0, 0.0)
    ones_k = jnp.where((lane >= Q_C) & (lane < Q_C + 3), 1.0, 0.0)
    parts = _split3(c_ref[0])

    def head(ref, h):
        x = ref[0, :, (h // 2) * LANES:(h // 2 + 1) * LANES].astype(F32)
        return pltpu.roll(x, HEAD_DIM, axis=1) if h % 2 else x

    def spread(h, first_lane):
        out = jnp.where(lane == first_lane, parts[0][:, h:h + 1], 0.0)
        for j in (1, 2):
            out = out + jnp.where(lane == first_lane + j, parts[j][:, h:h + 1], 0.0)
        return out

    rows = sqa_ref.shape[3]
    for h in range(FOX_HEADS):
        cols = slice(h * LANES, (h + 1) * LANES)
        q = jnp.where(low, head(fq_ref, h) * scale, 0.0) + spread(h, Q_C) + ones_q
        k = jnp.where(low, head(fk_ref, h), 0.0) - spread(h, K_C) + ones_k
        fqa_ref[0, :, cols] = q.astype(BF16)
        fka_ref[0, :, cols] = k.astype(BF16)
        fva_ref[0, :, cols] = jnp.where(low, head(fv_ref, h), 1.0).astype(BF16)
        sq = jnp.where(low, head(sq_ref, h) * scale, 0.0).astype(BF16)
        for j in range(sqa_ref.shape[1]):
            sqa_ref[0, j, h] = sq[j * rows:(j + 1) * rows]
    for g in range(SWA_KV_HEADS):
        cols = slice(g * LANES, (g + 1) * LANES)
        kv_w = SWA_KV_HEADS * LANES
        ska_ref[0, :, cols] = jnp.where(low, skv_ref[0, :, cols], jnp.zeros((), BF16))
        sva_ref[0, :, cols] = jnp.where(low, skv_ref[0, :, kv_w + g * LANES:kv_w + (g + 1) * LANES],
                                        jnp.ones((), BF16))


def _attention_prep(proj3, c, tp=ATT_TILE):
    b, s, _ = proj3.shape
    w = SWA_WINDOW
    kv_w = SWA_KV_HEADS * LANES
    col = lambda blk: (lambda i, j: (i, j, blk))
    wide = FOX_HEADS * LANES
    return pl.pallas_call(
        _prep_kernel,
        grid=(b, s // tp),
        in_specs=[
            pl.BlockSpec((1, tp, GROUP_W), col(COL_FQ)),
            pl.BlockSpec((1, tp, GROUP_W), col(COL_FK)),
            pl.BlockSpec((1, tp, GROUP_W), col(COL_FV)),
            pl.BlockSpec((1, tp, GROUP_W), col(COL_SQ)),
            pl.BlockSpec((1, tp, GROUP_W), col(COL_SKV)),
            pl.BlockSpec((1, tp, LANES), col(0)),
        ],
        out_specs=[
            pl.BlockSpec((1, tp, wide), col(0)),
            pl.BlockSpec((1, tp, wide), col(0)),
            pl.BlockSpec((1, tp, wide), col(0)),
            pl.BlockSpec((1, tp // w, SWA_HEADS, w, LANES), lambda i, j: (i, j, 0, 0, 0)),
            pl.BlockSpec((1, tp, kv_w), col(0)),
            pl.BlockSpec((1, tp, kv_w), col(0)),
        ],
        out_shape=[
            jax.ShapeDtypeStruct((b, s, wide), BF16),
            jax.ShapeDtypeStruct((b, s, wide), BF16),
            jax.ShapeDtypeStruct((b, s, wide), BF16),
            jax.ShapeDtypeStruct((b, s // w, SWA_HEADS, w, LANES), BF16),
            jax.ShapeDtypeStruct((b, s, kv_w), BF16),
            jax.ShapeDtypeStruct((b, s, kv_w), BF16),
        ],
        compiler_params=_cparams(("parallel", "parallel")),
        name="attention_prep",
    )(proj3, proj3, proj3, proj3, proj3, c)


def _causal_pairs(n_tiles):
    qi, ki = [], []
    for q in range(n_tiles):
        for k in range(q + 1):
            qi.append(q)
            ki.append(k)
    return jnp.asarray(qi, jnp.int32), jnp.asarray(ki, jnp.int32)


def _half_mask(half):
    lane = _lane_iota()
    return (lane >= HEAD_DIM) if half else (lane < HEAD_DIM)


def _qk(q, k):
    return lax.dot_general(q, k, (((1,), (1,)), ((), ())), preferred_element_type=F32)


def _softmax_step(s, m_prev):
    m_new = jnp.maximum(m_prev, jnp.max(s, axis=1, keepdims=True))
    alpha = jnp.exp(m_prev - m_new)
    p = jnp.concatenate(
        [jnp.exp(s[:, j * LANES:(j + 1) * LANES] - m_new) for j in range(s.shape[1] // LANES)],
        axis=1)
    return m_new, alpha, p.astype(BF16)


def _divide_by_upper_lanes(acc):
    return acc / pltpu.roll(acc, HEAD_DIM, axis=1)


def _join_heads(o_even, o_odd):
    return jnp.where(_half_mask(0), o_even, pltpu.roll(o_odd, HEAD_DIM, axis=1))


def _fox_kernel(qi_ref, ki_ref, q_ref, k_ref, v_ref, mask_ref, o_ref, m_sc, acc_sc):
    p_id = pl.program_id(1)
    qi = qi_ref[p_id]
    ki = ki_ref[p_id]

    @pl.when(ki == 0)
    def _():
        m_sc[...] = jnp.full(m_sc.shape, NEG, F32)
        acc_sc[...] = jnp.zeros(acc_sc.shape, F32)

    def attend(masked):
        for h in range(FOX_HEADS):
            cols = slice(h * LANES, (h + 1) * LANES)
            s = _qk(q_ref[0, :, cols], k_ref[0, :, cols])
            if masked:
                s = s + mask_ref[...]
            m_new, alpha, p = _softmax_step(s, m_sc[h])
            m_sc[h] = m_new
            acc_sc[h] = alpha * acc_sc[h] + jnp.dot(p, v_ref[0, :, cols],
                                                    preferred_element_type=F32)

    @pl.when(ki < qi)
    def _():
        attend(False)

    @pl.when(ki == qi)
    def _():
        attend(True)
        for hp in range(FOX_HEADS // 2):
            o = _join_heads(_divide_by_upper_lanes(acc_sc[2 * hp]),
                            _divide_by_upper_lanes(acc_sc[2 * hp + 1]))
            o_ref[0, :, hp * LANES:(hp + 1) * LANES] = o.astype(BF16)


def _causal_mask_tile(t):
    r = lax.broadcasted_iota(jnp.int32, (t, t), 0)
    c = lax.broadcasted_iota(jnp.int32, (t, t), 1)
    return jnp.where(c <= r, 0.0, NEG).astype(F32)


def _fox_attention(fqa, fka, fva, t=ATT_TILE):
    b, s, wide = fqa.shape
    qi, ki = _causal_pairs(s // t)
    grid_spec = pltpu.PrefetchScalarGridSpec(
        num_scalar_prefetch=2,
        grid=(b, qi.shape[0]),
        in_specs=[
            pl.BlockSpec((1, t, wide), lambda i, p, qi, ki: (i, qi[p], 0)),
            pl.BlockSpec((1, t, wide), lambda i, p, qi, ki: (i, ki[p], 0)),
            pl.BlockSpec((1, t, wide), lambda i, p, qi, ki: (i, ki[p], 0)),
            pl.BlockSpec((t, t), lambda i, p, qi, ki: (0, 0)),
        ],
        out_specs=pl.BlockSpec((1, t, GROUP_W), lambda i, p, qi, ki: (i, qi[p], 0)),
        scratch_shapes=[
            pltpu.VMEM((FOX_HEADS, t, LANES), F32),
            pltpu.VMEM((FOX_HEADS, t, LANES), F32),
        ],
    )
    return pl.pallas_call(
        _fox_kernel,
        grid_spec=grid_spec,
        out_shape=jax.ShapeDtypeStruct((b, s, GROUP_W), BF16),
        compiler_params=_cparams(("parallel", "arbitrary")),
        name="fox_attention",
    )(qi, ki, fqa, fka, fva, _causal_mask_tile(t))


def _diff_kernel(qi_ref, ki_ref, q_ref, k_ref, v_ref, bias_ref, lam_ref, g_ref,
                 o_ref, m_sc, acc_sc, *, lam_init):
    p_id = pl.program_id(1)
    qi = qi_ref[p_id]
    ki = ki_ref[p_id]
    scale = HEAD_DIM ** -0.5

    @pl.when(ki == 0)
    def _():
        m_sc[...] = jnp.full(m_sc.shape, NEG, F32)
        acc_sc[...] = jnp.zeros(acc_sc.shape, F32)

    def attend(near):
        for h in range(DIFF_HEADS):
            cols = slice(h * LANES, (h + 1) * LANES)
            q2 = q_ref[0, :, cols]
            k2 = k_ref[0, :, cols]
            v2 = v_ref[0, :, cols]
            v_ones = jnp.concatenate([v2, jnp.ones_like(v2)], axis=1)
            for mm in range(2):
                idx = 2 * h + mm
                qm = jnp.where(_half_mask(mm), q2, jnp.zeros_like(q2)) * scale
                s = _qk(qm, k2)
                if near:
                    s = s + bias_ref[h, qi - ki]
                m_new, alpha, p = _softmax_step(s, m_sc[idx])
                m_sc[idx] = m_new
                pv = jnp.dot(p, v_ones, preferred_element_type=F32)
                acc_sc[idx] = jnp.concatenate([alpha, alpha], axis=1) * acc_sc[idx] + pv

    @pl.when(qi - ki >= 2)
    def _():
        attend(False)

    @pl.when(qi - ki == 1)
    def _():
        attend(True)

    @pl.when(ki == qi)
    def _():
        attend(True)
        lv = lam_ref[...]
        lam = (jnp.exp(jnp.sum(lv[0:1] * lv[1:2], axis=1, keepdims=True))
               - jnp.exp(jnp.sum(lv[2:3] * lv[3:4], axis=1, keepdims=True)) + lam_init)
        for h in range(DIFF_HEADS):
            a1 = acc_sc[2 * h]
            a2 = acc_sc[2 * h + 1]
            o = a1[:, :LANES] / a1[:, LANES:] - lam * (a2[:, :LANES] / a2[:, LANES:])
            ms = jnp.mean(o * o, axis=1, keepdims=True)
            o = o * lax.rsqrt(ms + LN_EPS) * g_ref[...] * (1.0 - lam_init)
            o_ref[0, :, h * LANES:(h + 1) * LANES] = o.astype(BF16)


def _toeplitz(u, t):
    x = jnp.tile(u, (1, t + 1))[:, :2 * t * t].reshape(u.shape[0], t, 2 * t)
    return x[:, :, :t][:, :, ::-1]


def _diff_bias_tiles(diff_bias, t):
    assert t + 1 >= REL_MAX_DIST
    bt = diff_bias.T.astype(F32)
    bt = bt - bt[:, -1:]
    masked = jnp.full((bt.shape[0], t - 1), NEG, F32)
    diag = jnp.concatenate([masked, bt[:, :t]], axis=1)
    nxt = bt[:, 1:2 * t]
    return jnp.stack([_toeplitz(diag, t), _toeplitz(nxt, t)], axis=1)


def _diff_attention(proj3, bias_tiles, lam_vecs, g_pad, lam_init, t=ATT_TILE):
    b, s, _ = proj3.shape
    qi, ki = _causal_pairs(s // t)
    grid_spec = pltpu.PrefetchScalarGridSpec(
        num_scalar_prefetch=2,
        grid=(b, qi.shape[0]),
        in_specs=[
            pl.BlockSpec((1, t, GROUP_W), lambda i, p, qi, ki: (i, qi[p], COL_DQ)),
            pl.BlockSpec((1, t, GROUP_W), lambda i, p, qi, ki: (i, ki[p], COL_DK)),
            pl.BlockSpec((1, t, GROUP_W), lambda i, p, qi, ki: (i, ki[p], COL_DV)),
            pl.BlockSpec((DIFF_HEADS, 2, t, t), lambda i, p, qi, ki: (0, 0, 0, 0)),
            pl.BlockSpec((4, LANES), lambda i, p, qi, ki: (0, 0)),
            pl.BlockSpec((1, LANES), lambda i, p, qi, ki: (0, 0)),
        ],
        out_specs=pl.BlockSpec((1, t, GROUP_W), lambda i, p, qi, ki: (i, qi[p], 0)),
        scratch_shapes=[
            pltpu.VMEM((2 * DIFF_HEADS, t, LANES), F32),
            pltpu.VMEM((2 * DIFF_HEADS, t, 2 * LANES), F32),
        ],
    )
    return pl.pallas_call(
        functools.partial(_diff_kernel, lam_init=lam_init),
        grid_spec=grid_spec,
        out_shape=jax.ShapeDtypeStruct((b, s, GROUP_W), BF16),
        compiler_params=_cparams(("parallel", "arbitrary")),
        name="diff_attention",
    )(qi, ki, proj3, proj3, proj3, bias_tiles, lam_vecs, g_pad)


def _swa_kernel(q_ref, kc_ref, kp_ref, vc_ref, vp_ref, bias_ref, sink_ref, o_ref):
    n = pl.program_id(1)
    first = (n == 0).astype(jnp.int32)
    w = SWA_WINDOW
    low = _half_mask(0)
    for g in range(SWA_KV_HEADS):
        kv = slice(g * LANES, (g + 1) * LANES)
        q = q_ref[0, 0, g * SWA_GROUP:(g + 1) * SWA_GROUP].reshape(SWA_GROUP * w, LANES)
        k = jnp.concatenate([kp_ref[0, :, kv], kc_ref[0, :, kv]], axis=0)
        v = jnp.concatenate([vp_ref[0, :, kv], vc_ref[0, :, kv]], axis=0)
        s = _qk(q, k) + bias_ref[first, g]
        sink = sink_ref[g]
        m, _, e = _softmax_step(s, sink)
        tot = jnp.dot(e, v, preferred_element_type=F32) + jnp.where(low, 0.0, jnp.exp(sink - m))
        o = _divide_by_upper_lanes(tot)
        for j in range(SWA_GROUP // 2):
            pair = g * (SWA_GROUP // 2) + j
            o_ref[0, :, pair * LANES:(pair + 1) * LANES] = _join_heads(
                o[2 * j * w:(2 * j + 1) * w], o[(2 * j + 1) * w:(2 * j + 2) * w]).astype(BF16)


def _swa_bias_tiles(swa_bias):
    w = SWA_WINDOW
    bt = swa_bias.T.astype(F32)
    heads = bt.shape[0]
    prev = jnp.concatenate([bt[:, 1:], jnp.full((heads, w), NEG, F32)], axis=1)
    cur = jnp.concatenate([jnp.full((heads, w - 1), NEG, F32), bt], axis=1)
    prev_t = _toeplitz(prev, w)
    cur_t = _toeplitz(cur, w)
    sets = [jnp.concatenate([pt, cur_t], axis=2) for pt in (prev_t, jnp.full_like(prev_t, NEG))]
    return jnp.stack(sets).reshape(2, SWA_KV_HEADS, SWA_GROUP * w, 2 * w)


def _swa_attention(sqa, ska, sva, bias_tiles, sink_rows):
    b, nb, _, w, _ = sqa.shape
    kv_w = SWA_KV_HEADS * LANES
    cur = lambda i, n: (i, n, 0)
    prev = lambda i, n: (i, jnp.maximum(n - 1, 0), 0)
    return pl.pallas_call(
        _swa_kernel,
        grid=(b, nb),
        in_specs=[
            pl.BlockSpec((1, 1, SWA_HEADS, w, LANES), lambda i, n: (i, n, 0, 0, 0)),
            pl.BlockSpec((1, w, kv_w), cur),
            pl.BlockSpec((1, w, kv_w), prev),
            pl.BlockSpec((1, w, kv_w), cur),
            pl.BlockSpec((1, w, kv_w), prev),
            pl.BlockSpec((2, SWA_KV_HEADS, SWA_GROUP * w, 2 * w), lambda i, n: (0, 0, 0, 0)),
            pl.BlockSpec((SWA_KV_HEADS, SWA_GROUP * w, LANES), lambda i, n: (0, 0, 0)),
        ],
        out_specs=pl.BlockSpec((1, w, GROUP_W), cur),
        out_shape=jax.ShapeDtypeStruct((b, nb * w, GROUP_W), BF16),
        compiler_params=_cparams(("parallel", "arbitrary")),
        name="swa_attention",
    )(sqa, ska, ska, sva, sva, bias_tiles, sink_rows)


def _pool_kernel(u_ref, w_ref, sc_ref, o_ref):
    s_len = u_ref.shape[1]
    row = lax.broadcasted_iota(jnp.int32, (s_len, POOL_CH), 0)
    for g in range(POOL_GROUPS):
        cols = slice(g * POOL_CH, (g + 1) * POOL_CH)
        u = u_ref[0, :, cols].astype(F32)
        win = u
        shift = 1
        while shift < POOL_WINDOWS[g]:
            win = win + jnp.where(row >= shift, pltpu.roll(win, shift, axis=0), 0.0)
            shift *= 2
        cnt = jnp.minimum(row + 1, POOL_WINDOWS[g]).astype(F32)
        pooled = (win / cnt - u).astype(BF16)
        y = jnp.dot(pooled, w_ref[g], preferred_element_type=F32) * sc_ref[:, cols]
        o_ref[0, :, cols] = y.astype(BF16)


def _pool_mixer(proj3, pool_w_bf, pool_scale_row):
    b, s, _ = proj3.shape
    return pl.pallas_call(
        _pool_kernel,
        grid=(b,),
        in_specs=[
            pl.BlockSpec((1, s, GROUP_W), lambda i: (i, 0, COL_PU)),
            pl.BlockSpec((POOL_GROUPS, POOL_CH, POOL_CH), lambda i: (0, 0, 0)),
            pl.BlockSpec((1, GROUP_W), lambda i: (0, 0)),
        ],
        out_specs=pl.BlockSpec((1, s, GROUP_W), lambda i: (i, 0, 0)),
        out_shape=jax.ShapeDtypeStruct((b, s, GROUP_W), BF16),
        compiler_params=_cparams(("parallel",)),
        name="pool_mixer",
    )(proj3, pool_w_bf, pool_scale_row)


def _layer_norm(z, g, b):
    mu = jnp.mean(z, axis=1, keepdims=True)
    zc = z - mu
    var = jnp.mean(zc * zc, axis=1, keepdims=True)
    return zc * lax.rsqrt(var + LN_EPS) * g + b


def _outproj_kernel(yf_ref, yd_ref, yp_ref, ys_ref, x_ref, wo_ref, g_ref, b_ref, wr_ref, rb_ref,
                    h32_ref, h16_ref, rl_ref):
    acc = jnp.dot(yf_ref[...], wo_ref[0], preferred_element_type=F32)
    acc += jnp.dot(yd_ref[...], wo_ref[1], preferred_element_type=F32)
    acc += jnp.dot(yp_ref[...], wo_ref[2], preferred_element_type=F32)
    acc += jnp.dot(ys_ref[...], wo_ref[3], preferred_element_type=F32)
    h = _layer_norm(ALPHA * x_ref[...] + acc, g_ref[...], b_ref[...])
    h32_ref[...] = h
    hb = h.astype(BF16)
    h16_ref[...] = hb
    rl_ref[...] = jnp.dot(hb, wr_ref[...], preferred_element_type=F32) + rb_ref[...]


def _outproj_ln(ys, x2d, wo4, g, b, wr, rb, tm=256):
    t, d = x2d.shape
    row = lambda i: (i, 0)
    const2 = lambda i: (0, 0)
    once = pl.Buffered(1)
    return pl.pallas_call(
        _outproj_kernel,
        grid=(t // tm,),
        in_specs=[pl.BlockSpec((tm, GROUP_W), row)] * 4 + [
            pl.BlockSpec((tm, d), row),
            pl.BlockSpec((4, GROUP_W, d), lambda i: (0, 0, 0), pipeline_mode=once),
            pl.BlockSpec((1, d), const2),
            pl.BlockSpec((1, d), const2),
            pl.BlockSpec((d, LANES), const2, pipeline_mode=once),
            pl.BlockSpec((1, LANES), const2),
        ],
        out_specs=[
            pl.BlockSpec((tm, d), row),
            pl.BlockSpec((tm, d), row),
            pl.BlockSpec((tm, LANES), row),
        ],
        out_shape=[
            jax.ShapeDtypeStruct((t, d), F32),
            jax.ShapeDtypeStruct((t, d), BF16),
            jax.ShapeDtypeStruct((t, LANES), F32),
        ],
        compiler_params=_cparams(("parallel",)),
        name="outproj_ln",
    )(*ys, x2d, wo4, g, b, wr, rb)


def _first_max(vals):
    best = vals[0]
    for v in vals[1:]:
        best = jnp.maximum(best, v)
    idx = jnp.full(best.shape, len(vals) - 1, jnp.int32)
    for j in range(len(vals) - 2, -1, -1):
        idx = jnp.where(vals[j] == best, j, idx)
    return best, idx


def _router_kernel(rl_ref, o_ref):
    lt = rl_ref[...].T
    gl = [lt[g:g + 1] for g in range(MOE_GROUPS)]
    gmax, gsel = _first_max(gl)
    gsum = gl[0] * 0.0
    for v in gl:
        gsum = gsum + jnp.exp(v - gmax)
    g_p = 1.0 / gsum
    e_in = []
    for j in range(EXPERTS_PER_GROUP):
        v = lt[MOE_GROUPS + j:MOE_GROUPS + j + 1]
        for g in range(1, MOE_GROUPS):
            row = MOE_GROUPS + g * EXPERTS_PER_GROUP + j
            v = jnp.where(gsel == g, lt[row:row + 1], v)
        e_in.append(v)
    v1, i1 = _first_max(e_in)
    rest = [jnp.where(i1 == j, NEG, e_in[j]) for j in range(EXPERTS_PER_GROUP)]
    v2, i2 = _first_max(rest)
    e21 = jnp.exp(v2 - v1)
    w1 = g_p / (1.0 + e21)
    w2 = g_p * e21 / (1.0 + e21)
    sub = lax.broadcasted_iota(jnp.int32, lt.shape, 0)
    out = jnp.zeros(lt.shape, F32)
    for e in range(N_EXPERTS):
        g, j = divmod(e, EXPERTS_PER_GROUP)
        gate = jnp.where(gsel == g, jnp.where(i1 == j, w1, jnp.where(i2 == j, w2, 0.0)), 0.0)
        out = jnp.where(sub == e, gate, out)
    o_ref[...] = out.T


def _router(rlog, tr=1024):
    t = rlog.shape[0]
    return pl.pallas_call(
        _router_kernel,
        grid=(t // tr,),
        in_specs=[pl.BlockSpec((tr, LANES), lambda i: (i, 0))],
        out_specs=pl.BlockSpec((tr, LANES), lambda i: (i, 0)),
        out_shape=jax.ShapeDtypeStruct((t, LANES), F32),
        compiler_params=_cparams(("parallel",)),
        name="router",
    )(rlog)


def _moe_kernel(h_ref, r_ref, wg_ref, wu_ref, wd_ref, y_ref, acc_sc):
    e = pl.program_id(1)

    @pl.when(e == 0)
    def _():
        acc_sc[...] = jnp.zeros(acc_sc.shape, F32)

    h = h_ref[...]
    a = jnp.dot(h, wg_ref[0], preferred_element_type=F32)
    u = jnp.dot(h, wu_ref[0], preferred_element_type=F32)
    lane = _lane_iota()
    gate = jnp.sum(jnp.where(lane == e, r_ref[...], 0.0), axis=1, keepdims=True)
    hid = (a / (1.0 + jnp.exp(-a))) * u * gate
    acc_sc[...] += jnp.dot(hid.astype(BF16), wd_ref[0], preferred_element_type=F32)

    @pl.when(e == pl.num_programs(1) - 1)
    def _():
        y_ref[...] = acc_sc[...]


def _moe(h16, route, wg, wu, wd, tm=512):
    t, d = h16.shape
    n_e, _, f = wg.shape
    return pl.pallas_call(
        _moe_kernel,
        grid=(t // tm, n_e),
        in_specs=[
            pl.BlockSpec((tm, d), lambda i, e: (i, 0)),
            pl.BlockSpec((tm, LANES), lambda i, e: (i, 0)),
            pl.BlockSpec((1, d, f), lambda i, e: (e, 0, 0)),
            pl.BlockSpec((1, d, f), lambda i, e: (e, 0, 0)),
            pl.BlockSpec((1, f, d), lambda i, e: (e, 0, 0)),
        ],
        out_specs=pl.BlockSpec((tm, d), lambda i, e: (i, 0)),
        out_shape=jax.ShapeDtypeStruct((t, d), F32),
        scratch_shapes=[pltpu.VMEM((tm, d), F32)],
        compiler_params=_cparams(("parallel", "arbitrary")),
        name="moe",
    )(h16, route, wg, wu, wd)


def _ple_kernel(h32_ref, h16_ref, y_ref, p_ref, wg_ref, wu_ref, g_ref, b_ref, o_ref):
    gate = jnp.dot(h16_ref[...], wg_ref[...], preferred_element_type=F32)
    up = jnp.dot(p_ref[...].astype(BF16), wu_ref[...], preferred_element_type=F32)
    ple = up / (1.0 + jnp.exp(-gate))
    o_ref[...] = _layer_norm(ALPHA * h32_ref[...] + y_ref[...] + ple, g_ref[...], b_ref[...])


def _ple_ln(h32, h16, y, p2d, wg, wu, g, b, tm=256):
    t, d = h32.shape
    row = lambda i: (i, 0)
    const2 = lambda i: (0, 0)
    once = pl.Buffered(1)
    return pl.pallas_call(
        _ple_kernel,
        grid=(t // tm,),
        in_specs=[
            pl.BlockSpec((tm, d), row),
            pl.BlockSpec((tm, d), row),
            pl.BlockSpec((tm, d), row),
            pl.BlockSpec((tm, PLE_DIM), row),
            pl.BlockSpec((d, d), const2, pipeline_mode=once),
            pl.BlockSpec((PLE_DIM, d), const2, pipeline_mode=once),
            pl.BlockSpec((1, d), const2),
            pl.BlockSpec((1, d), const2),
        ],
        out_specs=pl.BlockSpec((tm, d), row),
        out_shape=jax.ShapeDtypeStruct((t, d), F32),
        compiler_params=_cparams(("parallel",)),
        name="ple_ln",
    )(h32, h16, y, p2d, wg, wu, g, b)


def _t5_bucket(n):
    max_exact = REL_BUCKETS // 2
    nf = jnp.maximum(n, 1).astype(F32)
    large = max_exact + (jnp.log(nf / max_exact) / math.log(REL_MAX_DIST / max_exact)
                         * (REL_BUCKETS - max_exact)).astype(jnp.int32)
    large = jnp.minimum(large, REL_BUCKETS - 1)
    return jnp.where(n < max_exact, n, large)


def _pad_lanes(v, width=LANES):
    v = v.astype(F32).reshape(1, -1)
    return jnp.pad(v, ((0, 0), (0, width - v.shape[1])))


def _reorder_w_in(w):
    o_ff = 3 * GROUP_W
    o_d = o_ff + FOX_HEADS
    o_sk = o_d + 5 * GROUP_W
    o_sv = o_sk + SWA_KV_HEADS * HEAD_DIM
    sk = [w[:, o_sk + g * HEAD_DIM:o_sk + (g + 1) * HEAD_DIM] for g in range(SWA_KV_HEADS)]
    sv = [w[:, o_sv + g * HEAD_DIM:o_sv + (g + 1) * HEAD_DIM] for g in range(SWA_KV_HEADS)]
    main = jnp.concatenate(
        [w[:, :o_ff], w[:, o_d:o_sk]] + [m for g in sk for m in (g, g)] + [m for g in sv for m in (g, g)],
        axis=1)
    w_ff = jnp.pad(w[:, o_ff:o_d], ((0, 0), (0, LANES - FOX_HEADS)))
    return main.astype(BF16), w_ff.astype(BF16)


def kernel(x, p, rel_table, w_in, b_f, lam_q1, lam_k1, lam_q2, lam_k2, diff_norm_g, pool_w, pool_scale, sinks, w_o, ln1_g, ln1_b, router_g_w, router_g_b, router_e_w, router_e_b, w_gate, w_up, w_down, ple_gate_w, ple_up_w, ln2_g, ln2_b):
    b, s, d = x.shape
    t = b * s
    depth = w_in.shape[0]
    assert d == D_MODEL and s % (2 * ATT_TILE) == 0 and t % 1024 == 0

    bias_by_dist = rel_table[_t5_bucket(jnp.arange(s, dtype=jnp.int32))]
    diff_tiles = _diff_bias_tiles(bias_by_dist[:, :DIFF_HEADS], ATT_TILE)
    swa_tiles = _swa_bias_tiles(bias_by_dist[:SWA_WINDOW, DIFF_HEADS:])

    x2d = x.reshape(t, d)
    for i in range(depth):
        w_main, w_ff = _reorder_w_in(w_in[i])
        proj, ff = _inproj(x2d, w_main, w_ff)
        proj3 = proj.reshape(b, s, PROJ_W)

        c = _gate_cumsum(ff.reshape(b, s, LANES), _pad_lanes(b_f[i]))
        fqa, fka, fva, sqa, ska, sva = _attention_prep(proj3, c)
        y_fox = _fox_attention(fqa, fka, fva)

        lam_init = 0.8 - 0.6 * math.exp(-0.3 * i)
        lam_vecs = jnp.concatenate(
            [_pad_lanes(v) for v in (lam_q1[i], lam_k1[i], lam_q2[i], lam_k2[i])], axis=0)
        y_diff = _diff_attention(proj3, diff_tiles, lam_vecs, _pad_lanes(diff_norm_g[i]), lam_init)

        y_pool = _pool_mixer(proj3, pool_w[i].astype(BF16), pool_scale[i].reshape(1, GROUP_W))
        sink_rows = jnp.broadcast_to(
            sinks[i].astype(F32).reshape(SWA_KV_HEADS, SWA_GROUP, 1, 1),
            (SWA_KV_HEADS, SWA_GROUP, SWA_WINDOW, LANES)).reshape(SWA_KV_HEADS, -1, LANES)
        y_swa = _swa_attention(sqa, ska, sva, swa_tiles, sink_rows)

        w_r = jnp.concatenate([router_g_w[i], router_e_w[i]], axis=1)
        w_r = jnp.pad(w_r, ((0, 0), (0, LANES - w_r.shape[1]))).astype(BF16)
        r_b = _pad_lanes(jnp.concatenate([router_g_b[i], router_e_b[i]]))
        ys = [y.reshape(t, GROUP_W) for y in (y_fox, y_diff, y_pool, y_swa)]
        h32, h16, rlog = _outproj_ln(
            ys, x2d, w_o[i].astype(BF16).reshape(4, GROUP_W, d),
            ln1_g[i].reshape(1, d), ln1_b[i].reshape(1, d), w_r, r_b)

        route = _router(rlog)
        y_moe = _moe(h16, route, w_gate[i].astype(BF16), w_up[i].astype(BF16),
                     w_down[i].astype(BF16))
        x2d = _ple_ln(h32, h16, y_moe, p[i].reshape(t, PLE_DIM), ple_gate_w[i].astype(BF16),
                      ple_up_w[i].astype(BF16), ln2_g[i].reshape(1, d), ln2_b[i].reshape(1, d))
    return x2d.reshape(b, s, d)
```

```python
import functools
import math

import jax
import jax.numpy as jnp
from jax import lax
from jax.experimental import pallas as pl
from jax.experimental.pallas import tpu as pltpu

F32 = jnp.float32
BF16 = jnp.bfloat16

D_MODEL = 2048
HEAD_DIM = 64
GROUP_W = D_MODEL // 4
FOX_HEADS = GROUP_W // HEAD_DIM
DIFF_HEADS = GROUP_W // (2 * HEAD_DIM)
POOL_GROUPS = 4
POOL_CH = GROUP_W // POOL_GROUPS
POOL_WINDOWS = (2, 4, 8, 16)
SWA_HEADS = GROUP_W // HEAD_DIM
SWA_KV_HEADS = 2
SWA_GROUP = SWA_HEADS // SWA_KV_HEADS
SWA_WINDOW = 128
REL_BUCKETS = 32
REL_MAX_DIST = 128
MOE_GROUPS = 4
EXPERTS_PER_GROUP = 4
N_EXPERTS = MOE_GROUPS * EXPERTS_PER_GROUP
D_EXPERT = D_MODEL // 4
PLE_DIM = 256
DEPTH = 2
ALPHA = (2 * DEPTH) ** 0.25
LN_EPS = 1e-5

LANES = 128
VMEM_LIMIT = 56 * 1024 * 1024

PROJ_W = 9 * GROUP_W
COL_FQ, COL_FK, COL_FV, COL_DQ, COL_DK, COL_DV, COL_PU, COL_SQ, COL_SKV = range(9)

ATT_TILE = 256
NEG = -1e30

Q_C = HEAD_DIM
K_C = HEAD_DIM + 3


def _cparams(sem, vmem=VMEM_LIMIT):
    return pltpu.CompilerParams(dimension_semantics=sem, vmem_limit_bytes=vmem)


def _inproj_kernel(x_ref, w_ref, wff_ref, o_ref, ff_ref, xb_ref):
    j = pl.program_id(1)

    @pl.when(j == 0)
    def _():
        xb_ref[...] = x_ref[...].astype(BF16)
        ff_ref[...] = jnp.dot(xb_ref[...], wff_ref[...], preferred_element_type=F32)

    o_ref[...] = jnp.dot(xb_ref[...], w_ref[...], preferred_element_type=F32).astype(BF16)


def _inproj(x2d, w_main, w_ff, tm=1024, tn=GROUP_W):
    t, d = x2d.shape
    n = w_main.shape[1]
    tm = min(tm, t)
    return pl.pallas_call(
        _inproj_kernel,
        grid=(t // tm, n // tn),
        in_specs=[
            pl.BlockSpec((tm, d), lambda i, j: (i, 0)),
            pl.BlockSpec((d, tn), lambda i, j: (0, j)),
            pl.BlockSpec((d, LANES), lambda i, j: (0, 0)),
        ],
        out_specs=[
            pl.BlockSpec((tm, tn), lambda i, j: (i, j)),
            pl.BlockSpec((tm, LANES), lambda i, j: (i, 0)),
        ],
        out_shape=[
            jax.ShapeDtypeStruct((t, n), BF16),
            jax.ShapeDtypeStruct((t, LANES), F32),
        ],
        scratch_shapes=[pltpu.VMEM((tm, d), BF16)],
        compiler_params=_cparams(("parallel", "arbitrary")),
        name="inproj",
    )(x2d, w_main, w_ff)


def _gate_kernel(ff_ref, bf_ref, c_ref):
    f = ff_ref[0] + bf_ref[...]
    ls = jnp.minimum(f, 0.0) - jnp.log(1.0 + jnp.exp(-jnp.abs(f)))
    c = ls.T
    s_len = c.shape[1]
    lane = lax.broadcasted_iota(jnp.int32, c.shape, 1)
    shift = 1
    while shift < s_len:
        c = c + jnp.where(lane >= shift, pltpu.roll(c, shift, axis=1), 0.0)
        shift *= 2
    c_ref[0] = c.T


def _gate_cumsum(ff3, bf_pad):
    b, s, _ = ff3.shape
    return pl.pallas_call(
        _gate_kernel,
        grid=(b,),
        in_specs=[
            pl.BlockSpec((1, s, LANES), lambda i: (i, 0, 0)),
            pl.BlockSpec((1, LANES), lambda i: (0, 0)),
        ],
        out_specs=pl.BlockSpec((1, s, LANES), lambda i: (i, 0, 0)),
        out_shape=jax.ShapeDtypeStruct((b, s, LANES), F32),
        compiler_params=_cparams(("parallel",)),
        name="gate_cumsum",
    )(ff3, bf_pad)


def _lane_iota():
    return lax.broadcasted_iota(jnp.int32, (1, LANES), 1)


def _split3(c):
    hi = c.astype(BF16).astype(F32)
    r = c - hi
    lo = r.astype(BF16).astype(F32)
    lo2 = (r - lo).astype(BF16).astype(F32)
    return hi, lo, lo2


def _prep_kernel(fq_ref, fk_ref, fv_ref, sq_ref, skv_ref, c_ref,
                 fqa_ref, fka_ref, fva_ref, sqa_ref, ska_ref, sva_ref):
    scale = HEAD_DIM ** -0.5
    lane = _lane_iota()
    low = lane < HEAD_DIM
    ones_q = jnp.where((lane >= K_C) & (lane < K_C + 3), 1.0, 0.0)
    ones_k = jnp.where((lane >= Q_C) & (lane < Q_C + 3), 1.0, 0.0)
    parts = _split3(c_ref[0])

    def head(ref, h):
        x = ref[0, :, (h // 2) * LANES:(h // 2 + 1) * LANES].astype(F32)
        return pltpu.roll(x, HEAD_DIM, axis=1) if h % 2 else x

    def spread(h, first_lane):
        out = jnp.where(lane == first_lane, parts[0][:, h:h + 1], 0.0)
        for j in (1, 2):
            out = out + jnp.where(lane == first_lane + j, parts[j][:, h:h + 1], 0.0)
        return out

    rows = sqa_ref.shape[3]
    for h in range(FOX_HEADS):
        cols = slice(h * LANES, (h + 1) * LANES)
        q = jnp.where(low, head(fq_ref, h) * scale, 0.0) + spread(h, Q_C) + ones_q
        k = jnp.where(low, head(fk_ref, h), 0.0) - spread(h, K_C) + ones_k
        fqa_ref[0, :, cols] = q.astype(BF16)
        fka_ref[0, :, cols] = k.astype(BF16)
        fva_ref[0, :, cols] = jnp.where(low, head(fv_ref, h), 1.0).astype(BF16)
        sq = jnp.where(low, head(sq_ref, h) * scale, 0.0).astype(BF16)
        for j in range(sqa_ref.shape[1]):
            sqa_ref[0, j, h] = sq[j * rows:(j + 1) * rows]
    for g in range(SWA_KV_HEADS):
        cols = slice(g * LANES, (g + 1) * LANES)
        kv_w = SWA_KV_HEADS * LANES
        ska_ref[0, :, cols] = jnp.where(low, skv_ref[0, :, cols], jnp.zeros((), BF16))
        sva_ref[0, :, cols] = jnp.where(low, skv_ref[0, :, kv_w + g * LANES:kv_w + (g + 1) * LANES],
                                        jnp.ones((), BF16))


def _attention_prep(proj3, c, tp=ATT_TILE):
    b, s, _ = proj3.shape
    w = SWA_WINDOW
    kv_w = SWA_KV_HEADS * LANES
    col = lambda blk: (lambda i, j: (i, j, blk))
    wide = FOX_HEADS * LANES
    return pl.pallas_call(
        _prep_kernel,
        grid=(b, s // tp),
        in_specs=[
            pl.BlockSpec((1, tp, GROUP_W), col(COL_FQ)),
            pl.BlockSpec((1, tp, GROUP_W), col(COL_FK)),
            pl.BlockSpec((1, tp, GROUP_W), col(COL_FV)),
            pl.BlockSpec((1, tp, GROUP_W), col(COL_SQ)),
            pl.BlockSpec((1, tp, GROUP_W), col(COL_SKV)),
            pl.BlockSpec((1, tp, LANES), col(0)),
        ],
        out_specs=[
            pl.BlockSpec((1, tp, wide), col(0)),
            pl.BlockSpec((1, tp, wide), col(0)),
            pl.BlockSpec((1, tp, wide), col(0)),
            pl.BlockSpec((1, tp // w, SWA_HEADS, w, LANES), lambda i, j: (i, j, 0, 0, 0)),
            pl.BlockSpec((1, tp, kv_w), col(0)),
            pl.BlockSpec((1, tp, kv_w), col(0)),
        ],
        out_shape=[
            jax.ShapeDtypeStruct((b, s, wide), BF16),
            jax.ShapeDtypeStruct((b, s, wide), BF16),
            jax.ShapeDtypeStruct((b, s, wide), BF16),
            jax.ShapeDtypeStruct((b, s // w, SWA_HEADS, w, LANES), BF16),
            jax.ShapeDtypeStruct((b, s, kv_w), BF16),
            jax.ShapeDtypeStruct((b, s, kv_w), BF16),
        ],
        compiler_params=_cparams(("parallel", "parallel")),
        name="attention_prep",
    )(proj3, proj3, proj3, proj3, proj3, c)


def _causal_pairs(n_tiles):
    qi, ki = [], []
    for q in range(n_tiles):
        for k in range(q + 1):
            qi.append(q)
            ki.append(k)
    return jnp.asarray(qi, jnp.int32), jnp.asarray(ki, jnp.int32)


def _half_mask(half):
    lane = _lane_iota()
    return (lane >= HEAD_DIM) if half else (lane < HEAD_DIM)


def _qk(q, k):
    return lax.dot_general(q, k, (((1,), (1,)), ((), ())), preferred_element_type=F32)


def _softmax_step(s, m_prev):
    m_new = jnp.maximum(m_prev, jnp.max(s, axis=1, keepdims=True))
    alpha = jnp.exp(m_prev - m_new)
    p = jnp.concatenate(
        [jnp.exp(s[:, j * LANES:(j + 1) * LANES] - m_new) for j in range(s.shape[1] // LANES)],
        axis=1)
    return m_new, alpha, p.astype(BF16)


def _divide_by_upper_lanes(acc):
    return acc / pltpu.roll(acc, HEAD_DIM, axis=1)


def _join_heads(o_even, o_odd):
    return jnp.where(_half_mask(0), o_even, pltpu.roll(o_odd, HEAD_DIM, axis=1))


def _fox_kernel(qi_ref, ki_ref, q_ref, k_ref, v_ref, mask_ref, o_ref, m_sc, acc_sc):
    p_id = pl.program_id(1)
    qi = qi_ref[p_id]
    ki = ki_ref[p_id]

    @pl.when(ki == 0)
    def _():
        m_sc[...] = jnp.full(m_sc.shape, NEG, F32)
        acc_sc[...] = jnp.zeros(acc_sc.shape, F32)

    def attend(masked):
        for h in range(FOX_HEADS):
            cols = slice(h * LANES, (h + 1) * LANES)
            s = _qk(q_ref[0, :, cols], k_ref[0, :, cols])
            if masked:
                s = s + mask_ref[...]
            m_new, alpha, p = _softmax_step(s, m_sc[h])
            m_sc[h] = m_new
            acc_sc[h] = alpha * acc_sc[h] + jnp.dot(p, v_ref[0, :, cols],
                                                    preferred_element_type=F32)

    @pl.when(ki < qi)
    def _():
        attend(False)

    @pl.when(ki == qi)
    def _():
        attend(True)
        for hp in range(FOX_HEADS // 2):
            o = _join_heads(_divide_by_upper_lanes(acc_sc[2 * hp]),
                            _divide_by_upper_lanes(acc_sc[2 * hp + 1]))
            o_ref[0, :, hp * LANES:(hp + 1) * LANES] = o.astype(BF16)


def _causal_mask_tile(t):
    r = lax.broadcasted_iota(jnp.int32, (t, t), 0)
    c = lax.broadcasted_iota(jnp.int32, (t, t), 1)
    return jnp.where(c <= r, 0.0, NEG).astype(F32)


def _fox_attention(fqa, fka, fva, t=ATT_TILE):
    b, s, wide = fqa.shape
    qi, ki = _causal_pairs(s // t)
    grid_spec = pltpu.PrefetchScalarGridSpec(
        num_scalar_prefetch=2,
        grid=(b, qi.shape[0]),
        in_specs=[
            pl.BlockSpec((1, t, wide), lambda i, p, qi, ki: (i, qi[p], 0)),
            pl.BlockSpec((1, t, wide), lambda i, p, qi, ki: (i, ki[p], 0)),
            pl.BlockSpec((1, t, wide), lambda i, p, qi, ki: (i, ki[p], 0)),
            pl.BlockSpec((t, t), lambda i, p, qi, ki: (0, 0)),
        ],
        out_specs=pl.BlockSpec((1, t, GROUP_W), lambda i, p, qi, ki: (i, qi[p], 0)),
        scratch_shapes=[
            pltpu.VMEM((FOX_HEADS, t, LANES), F32),
            pltpu.VMEM((FOX_HEADS, t, LANES), F32),
        ],
    )
    return pl.pallas_call(
        _fox_kernel,
        grid_spec=grid_spec,
        out_shape=jax.ShapeDtypeStruct((b, s, GROUP_W), BF16),
        compiler_params=_cparams(("parallel", "arbitrary")),
        name="fox_attention",
    )(qi, ki, fqa, fka, fva, _causal_mask_tile(t))


def _diff_kernel(qi_ref, ki_ref, q_ref, k_ref, v_ref, bias_ref, lam_ref, g_ref,
                 o_ref, m_sc, acc_sc, *, lam_init):
    p_id = pl.program_id(1)
    qi = qi_ref[p_id]
    ki = ki_ref[p_id]
    scale = HEAD_DIM ** -0.5

    @pl.when(ki == 0)
    def _():
        m_sc[...] = jnp.full(m_sc.shape, NEG, F32)
        acc_sc[...] = jnp.zeros(acc_sc.shape, F32)

    def attend(near):
        for h in range(DIFF_HEADS):
            cols = slice(h * LANES, (h + 1) * LANES)
            q2 = q_ref[0, :, cols]
            k2 = k_ref[0, :, cols]
            v2 = v_ref[0, :, cols]
            v_ones = jnp.concatenate([v2, jnp.ones_like(v2)], axis=1)
            for mm in range(2):
                idx = 2 * h + mm
                qm = jnp.where(_half_mask(mm), q2, jnp.zeros_like(q2)) * scale
                s = _qk(qm, k2)
                if near:
                    s = s + bias_ref[h, qi - ki]
                m_new, alpha, p = _softmax_step(s, m_sc[idx])
                m_sc[idx] = m_new
                pv = jnp.dot(p, v_ones, preferred_element_type=F32)
                acc_sc[idx] = jnp.concatenate([alpha, alpha], axis=1) * acc_sc[idx] + pv

    @pl.when(qi - ki >= 2)
    def _():
        attend(False)

    @pl.when(qi - ki == 1)
    def _():
        attend(True)

    @pl.when(ki == qi)
    def _():
        attend(True)
        lv = lam_ref[...]
        lam = (jnp.exp(jnp.sum(lv[0:1] * lv[1:2], axis=1, keepdims=True))
               - jnp.exp(jnp.sum(lv[2:3] * lv[3:4], axis=1, keepdims=True)) + lam_init)
        for h in range(DIFF_HEADS):
            a1 = acc_sc[2 * h]
            a2 = acc_sc[2 * h + 1]
            o = a1[:, :LANES] / a1[:, LANES:] - lam * (a2[:, :LANES] / a2[:, LANES:])
            ms = jnp.mean(o * o, axis=1, keepdims=True)
            o = o * lax.rsqrt(ms + LN_EPS) * g_ref[...] * (1.0 - lam_init)
            o_ref[0, :, h * LANES:(h + 1) * LANES] = o.astype(BF16)


def _toeplitz(u, t):
    x = jnp.tile(u, (1, t + 1))[:, :2 * t * t].reshape(u.shape[0], t, 2 * t)
    return x[:, :, :t][:, :, ::-1]


def _diff_bias_tiles(diff_bias, t):
    assert t + 1 >= REL_MAX_DIST
    bt = diff_bias.T.astype(F32)
    bt = bt - bt[:, -1:]
    masked = jnp.full((bt.shape[0], t - 1), NEG, F32)
    diag = jnp.concatenate([masked, bt[:, :t]], axis=1)
    nxt = bt[:, 1:2 * t]
    return jnp.stack([_toeplitz(diag, t), _toeplitz(nxt, t)], axis=1)


def _diff_attention(proj3, bias_tiles, lam_vecs, g_pad, lam_init, t=ATT_TILE):
    b, s, _ = proj3.shape
    qi, ki = _causal_pairs(s // t)
    grid_spec = pltpu.PrefetchScalarGridSpec(
        num_scalar_prefetch=2,
        grid=(b, qi.shape[0]),
        in_specs=[
            pl.BlockSpec((1, t, GROUP_W), lambda i, p, qi, ki: (i, qi[p], COL_DQ)),
            pl.BlockSpec((1, t, GROUP_W), lambda i, p, qi, ki: (i, ki[p], COL_DK)),
            pl.BlockSpec((1, t, GROUP_W), lambda i, p, qi, ki: (i, ki[p], COL_DV)),
            pl.BlockSpec((DIFF_HEADS, 2, t, t), lambda i, p, qi, ki: (0, 0, 0, 0)),
            pl.BlockSpec((4, LANES), lambda i, p, qi, ki: (0, 0)),
            pl.BlockSpec((1, LANES), lambda i, p, qi, ki: (0, 0)),
        ],
        out_specs=pl.BlockSpec((1, t, GROUP_W), lambda i, p, qi, ki: (i, qi[p], 0)),
        scratch_shapes=[
            pltpu.VMEM((2 * DIFF_HEADS, t, LANES), F32),
            pltpu.VMEM((2 * DIFF_HEADS, t, 2 * LANES), F32),
        ],
    )
    return pl.pallas_call(
        functools.partial(_diff_kernel, lam_init=lam_init),
        grid_spec=grid_spec,
        out_shape=jax.ShapeDtypeStruct((b, s, GROUP_W), BF16),
        compiler_params=_cparams(("parallel", "arbitrary")),
        name="diff_attention",
    )(qi, ki, proj3, proj3, proj3, bias_tiles, lam_vecs, g_pad)


def _swa_kernel(q_ref, kc_ref, kp_ref, vc_ref, vp_ref, bias_ref, sink_ref, o_ref):
    n = pl.program_id(1)
    first = (n == 0).astype(jnp.int32)
    w = SWA_WINDOW
    low = _half_mask(0)
    for g in range(SWA_KV_HEADS):
        kv = slice(g * LANES, (g + 1) * LANES)
        q = q_ref[0, 0, g * SWA_GROUP:(g + 1) * SWA_GROUP].reshape(SWA_GROUP * w, LANES)
        k = jnp.concatenate([kp_ref[0, :, kv], kc_ref[0, :, kv]], axis=0)
        v = jnp.concatenate([vp_ref[0, :, kv], vc_ref[0, :, kv]], axis=0)
        s = _qk(q, k) + bias_ref[first, g]
        sink = sink_ref[g]
        m, _, e = _softmax_step(s, sink)
        tot = jnp.dot(e, v, preferred_element_type=F32) + jnp.where(low, 0.0, jnp.exp(sink - m))
        o = _divide_by_upper_lanes(tot)
        for j in range(SWA_GROUP // 2):
            pair = g * (SWA_GROUP // 2) + j
            o_ref[0, :, pair * LANES:(pair + 1) * LANES] = _join_heads(
                o[2 * j * w:(2 * j + 1) * w], o[(2 * j + 1) * w:(2 * j + 2) * w]).astype(BF16)


def _swa_bias_tiles(swa_bias):
    w = SWA_WINDOW
    bt = swa_bias.T.astype(F32)
    heads = bt.shape[0]
    prev = jnp.concatenate([bt[:, 1:], jnp.full((heads, w), NEG, F32)], axis=1)
    cur = jnp.concatenate([jnp.full((heads, w - 1), NEG, F32), bt], axis=1)
    prev_t = _toeplitz(prev, w)
    cur_t = _toeplitz(cur, w)
    sets = [jnp.concatenate([pt, cur_t], axis=2) for pt in (prev_t, jnp.full_like(prev_t, NEG))]
    return jnp.stack(sets).reshape(2, SWA_KV_HEADS, SWA_GROUP * w, 2 * w)


def _swa_attention(sqa, ska, sva, bias_tiles, sink_rows):
    b, nb, _, w, _ = sqa.shape
    kv_w = SWA_KV_HEADS * LANES
    cur = lambda i, n: (i, n, 0)
    prev = lambda i, n: (i, jnp.maximum(n - 1, 0), 0)
    return pl.pallas_call(
        _swa_kernel,
        grid=(b, nb),
        in_specs=[
            pl.BlockSpec((1, 1, SWA_HEADS, w, LANES), lambda i, n: (i, n, 0, 0, 0)),
            pl.BlockSpec((1, w, kv_w), cur),
            pl.BlockSpec((1, w, kv_w), prev),
            pl.BlockSpec((1, w, kv_w), cur),
            pl.BlockSpec((1, w, kv_w), prev),
            pl.BlockSpec((2, SWA_KV_HEADS, SWA_GROUP * w, 2 * w), lambda i, n: (0, 0, 0, 0)),
            pl.BlockSpec((SWA_KV_HEADS, SWA_GROUP * w, LANES), lambda i, n: (0, 0, 0)),
        ],
        out_specs=pl.BlockSpec((1, w, GROUP_W), cur),
        out_shape=jax.ShapeDtypeStruct((b, nb * w, GROUP_W), BF16),
        compiler_params=_cparams(("parallel", "arbitrary")),
        name="swa_attention",
    )(sqa, ska, ska, sva, sva, bias_tiles, sink_rows)


def _pool_kernel(u_ref, w_ref, sc_ref, o_ref):
    s_len = u_ref.shape[1]
    row = lax.broadcasted_iota(jnp.int32, (s_len, POOL_CH), 0)
    for g in range(POOL_GROUPS):
        cols = slice(g * POOL_CH, (g + 1) * POOL_CH)
        u = u_ref[0, :, cols].astype(F32)
        win = u
        shift = 1
        while shift < POOL_WINDOWS[g]:
            win = win + jnp.where(row >= shift, pltpu.roll(win, shift, axis=0), 0.0)
            shift *= 2
        cnt = jnp.minimum(row + 1, POOL_WINDOWS[g]).astype(F32)
        pooled = (win / cnt - u).astype(BF16)
        y = jnp.dot(pooled, w_ref[g], preferred_element_type=F32) * sc_ref[:, cols]
        o_ref[0, :, cols] = y.astype(BF16)


def _pool_mixer(proj3, pool_w_bf, pool_scale_row):
    b, s, _ = proj3.shape
    return pl.pallas_call(
        _pool_kernel,
        grid=(b,),
        in_specs=[
            pl.BlockSpec((1, s, GROUP_W), lambda i: (i, 0, COL_PU)),
            pl.BlockSpec((POOL_GROUPS, POOL_CH, POOL_CH), lambda i: (0, 0, 0)),
            pl.BlockSpec((1, GROUP_W), lambda i: (0, 0)),
        ],
        out_specs=pl.BlockSpec((1, s, GROUP_W), lambda i: (i, 0, 0)),
        out_shape=jax.ShapeDtypeStruct((b, s, GROUP_W), BF16),
        compiler_params=_cparams(("parallel",)),
        name="pool_mixer",
    )(proj3, pool_w_bf, pool_scale_row)


def _layer_norm(z, g, b):
    mu = jnp.mean(z, axis=1, keepdims=True)
    zc = z - mu
    var = jnp.mean(zc * zc, axis=1, keepdims=True)
    return zc * lax.rsqrt(var + LN_EPS) * g + b


PAIRS = [(i, j) for i in range(EXPERTS_PER_GROUP) for j in range(i + 1, EXPERTS_PER_GROUP)]
N_BUCKETS = MOE_GROUPS * len(PAIRS)
ROUTE_BUCKET, ROUTE_W_LO, ROUTE_W_HI = 0, 1, 2
HX_W = D_MODEL + LANES


def _first_max(vals):
    best = vals[0]
    for v in vals[1:]:
        best = jnp.maximum(best, v)
    idx = jnp.full(best.shape, len(vals) - 1, jnp.int32)
    for j in range(len(vals) - 2, -1, -1):
        idx = jnp.where(vals[j] == best, j, idx)
    return best, idx


def _route_block(rl):
    lt = rl.T
    gl = [lt[g:g + 1] for g in range(MOE_GROUPS)]
    gmax, gsel = _first_max(gl)
    gsum = gl[0] * 0.0
    for v in gl:
        gsum = gsum + jnp.exp(v - gmax)
    g_p = 1.0 / gsum
    e_in = []
    for j in range(EXPERTS_PER_GROUP):
        v = lt[MOE_GROUPS + j:MOE_GROUPS + j + 1]
        for g in range(1, MOE_GROUPS):
            row = MOE_GROUPS + g * EXPERTS_PER_GROUP + j
            v = jnp.where(gsel == g, lt[row:row + 1], v)
        e_in.append(v)
    v1, i1 = _first_max(e_in)
    rest = [jnp.where(i1 == j, NEG, e_in[j]) for j in range(EXPERTS_PER_GROUP)]
    v2, i2 = _first_max(rest)
    e21 = jnp.exp(v2 - v1)
    w1 = g_p / (1.0 + e21)
    w2 = g_p * e21 / (1.0 + e21)
    lo = jnp.minimum(i1, i2)
    hi = jnp.maximum(i1, i2)
    pair = jnp.zeros_like(lo)
    for n, (a, b) in enumerate(PAIRS):
        pair = jnp.where((lo == a) & (hi == b), n, pair)
    bucket = (gsel * len(PAIRS) + pair).astype(F32)
    w_lo = jnp.where(i1 < i2, w1, w2)
    w_hi = jnp.where(i1 < i2, w2, w1)
    sub = lax.broadcasted_iota(jnp.int32, lt.shape, 0)
    out = jnp.where(sub == ROUTE_BUCKET, bucket,
                    jnp.where(sub == ROUTE_W_LO, w_lo, jnp.where(sub == ROUTE_W_HI, w_hi, 0.0)))
    return out.T


def _outproj_kernel(yf_ref, yd_ref, yp_ref, ys_ref, x_ref, wo_ref, g_ref, b_ref, wr_ref, rb_ref,
                    hx_ref):
    acc = jnp.dot(yf_ref[...], wo_ref[0], preferred_element_type=F32)
    acc += jnp.dot(yd_ref[...], wo_ref[1], preferred_element_type=F32)
    acc += jnp.dot(yp_ref[...], wo_ref[2], preferred_element_type=F32)
    acc += jnp.dot(ys_ref[...], wo_ref[3], preferred_element_type=F32)
    h = _layer_norm(ALPHA * x_ref[...] + acc, g_ref[...], b_ref[...])
    hx_ref[:, :D_MODEL] = h
    rl = jnp.dot(h.astype(BF16), wr_ref[...], preferred_element_type=F32) + rb_ref[...]
    hx_ref[:, D_MODEL:] = _route_block(rl)


def _outproj_ln(ys, x2d, wo4, g, b, wr, rb, tm=256):
    t, d = x2d.shape
    row = lambda i: (i, 0)
    const2 = lambda i: (0, 0)
    once = pl.Buffered(1)
    return pl.pallas_call(
        _outproj_kernel,
        grid=(t // tm,),
        in_specs=[pl.BlockSpec((tm, GROUP_W), row)] * 4 + [
            pl.BlockSpec((tm, d), row),
            pl.BlockSpec((4, GROUP_W, d), lambda i: (0, 0, 0), pipeline_mode=once),
            pl.BlockSpec((1, d), const2),
            pl.BlockSpec((1, d), const2),
            pl.BlockSpec((d, LANES), const2, pipeline_mode=once),
            pl.BlockSpec((1, LANES), const2),
        ],
        out_specs=pl.BlockSpec((tm, HX_W), row),
        out_shape=jax.ShapeDtypeStruct((t, HX_W), F32),
        compiler_params=_cparams(("parallel",)),
        name="outproj_ln",
    )(*ys, x2d, wo4, g, b, wr, rb)


MOE_TILE = 256


def _route_plan(bucket, tm, n_tiles):
    t = bucket.shape[0]
    onehot = (bucket[:, None] == jnp.arange(N_BUCKETS, dtype=jnp.int32)[None, :]).astype(jnp.int32)
    csum = jnp.cumsum(onehot, axis=0)
    rank = jnp.sum(onehot * csum, axis=1) - 1
    tiles_b = (csum[-1] + tm - 1) // tm
    tile_end = jnp.cumsum(tiles_b)
    slot0 = (tile_end - tiles_b) * tm
    pos = jnp.sum(onehot * slot0[None, :], axis=1) + rank
    src = jnp.full((n_tiles * tm,), -1, jnp.int32).at[pos].set(jnp.arange(t, dtype=jnp.int32))
    n_used = tile_end[-1:]
    tile = jnp.minimum(jnp.arange(n_tiles, dtype=jnp.int32), n_used - 1)
    tile_bucket = jnp.sum((tile[:, None] >= tile_end[None, :]).astype(jnp.int32), axis=1)
    group = tile_bucket // len(PAIRS)
    pair = tile_bucket % len(PAIRS)
    pair_lo = jnp.asarray([a for a, _ in PAIRS], jnp.int32)
    pair_hi = jnp.asarray([b for _, b in PAIRS], jnp.int32)
    sel = (pair[:, None] == jnp.arange(len(PAIRS), dtype=jnp.int32)[None, :]).astype(jnp.int32)
    e_lo = group * EXPERTS_PER_GROUP + jnp.sum(sel * pair_lo[None, :], axis=1)
    e_hi = group * EXPERTS_PER_GROUP + jnp.sum(sel * pair_hi[None, :], axis=1)
    return e_lo, e_hi, n_used, src


def _moe_kernel(elo_ref, ehi_ref, nused_ref, src_ref,
                hx_hbm, wgl_ref, wul_ref, wdl_ref, wgh_ref, wuh_ref, wdh_ref,
                y_hbm, hbuf, obuf, gsem, ssem, *, tm, t_rows):
    i = pl.program_id(0)
    n_used = nused_ref[0]
    slot = lax.rem(i, 2)

    def start_gather(tile, sl):
        def body(r, carry):
            row = jnp.maximum(src_ref[tile * tm + r], 0)
            pltpu.make_async_copy(hx_hbm.at[pl.ds(row, 1)], hbuf.at[sl, pl.ds(r, 1)],
                                  gsem.at[sl]).start()
            return carry
        lax.fori_loop(0, tm, body, 0, unroll=8)

    def start_scatter(tile, sl):
        def body(r, carry):
            row = src_ref[tile * tm + r]
            row = jnp.where(row < 0, t_rows + sl * tm + r, row)
            pltpu.make_async_copy(obuf.at[sl, pl.ds(r, 1)], y_hbm.at[pl.ds(row, 1)],
                                  ssem.at[sl]).start(priority=1)
            return carry
        lax.fori_loop(0, tm, body, 0, unroll=8)

    def wait_gather(sl):
        pltpu.make_async_copy(hx_hbm.at[pl.ds(0, tm)], hbuf.at[sl], gsem.at[sl]).wait()

    def wait_scatter(sl):
        pltpu.make_async_copy(obuf.at[sl], y_hbm.at[pl.ds(0, tm)], ssem.at[sl]).wait()

    @pl.when(i == 0)
    def _():
        start_gather(0, 0)
        obuf[...] = jnp.zeros(obuf.shape, F32)
        for sl in range(2):
            spare = pltpu.make_async_copy(obuf.at[sl], y_hbm.at[pl.ds(t_rows + sl * tm, tm)],
                                          ssem.at[sl])
            spare.start()
            spare.wait()

    @pl.when(i + 1 < n_used)
    def _():
        start_gather(i + 1, 1 - slot)

    @pl.when(i < n_used)
    def _():
        wait_gather(slot)
        x = hbuf[slot, :, :D_MODEL].astype(BF16)
        y = None
        for wg_ref, wu_ref, wd_ref, lane in ((wgl_ref, wul_ref, wdl_ref, ROUTE_W_LO),
                                             (wgh_ref, wuh_ref, wdh_ref, ROUTE_W_HI)):
            a = jnp.dot(x, wg_ref[0], preferred_element_type=F32)
            u = jnp.dot(x, wu_ref[0], preferred_element_type=F32)
            gate = hbuf[slot, :, D_MODEL + lane:D_MODEL + lane + 1]
            hid = (a / (1.0 + jnp.exp(-a))) * u * gate
            part = jnp.dot(hid.astype(BF16), wd_ref[0], preferred_element_type=F32)
            y = part if y is None else y + part

        @pl.when(i >= 2)
        def _():
            wait_scatter(slot)

        obuf[slot] = y
        start_scatter(i, slot)

    @pl.when(i == pl.num_programs(0) - 1)
    def _():
        wait_scatter(0)
        wait_scatter(1)


def _moe(hx, bucket, wg, wu, wd, tm=MOE_TILE):
    t = hx.shape[0]
    d = D_MODEL
    f = wg.shape[2]
    n_tiles = t // tm + N_BUCKETS
    e_lo, e_hi, n_used, src = _route_plan(bucket, tm, n_tiles)
    lo3 = lambda i, elo, ehi, nu, src: (elo[i], 0, 0)
    hi3 = lambda i, elo, ehi, nu, src: (ehi[i], 0, 0)
    grid_spec = pltpu.PrefetchScalarGridSpec(
        num_scalar_prefetch=4,
        grid=(n_tiles,),
        in_specs=[
            pl.BlockSpec(memory_space=pl.ANY),
            pl.BlockSpec((1, d, f), lo3),
            pl.BlockSpec((1, d, f), lo3),
            pl.BlockSpec((1, f, d), lo3),
            pl.BlockSpec((1, d, f), hi3),
            pl.BlockSpec((1, d, f), hi3),
            pl.BlockSpec((1, f, d), hi3),
        ],
        out_specs=pl.BlockSpec(memory_space=pl.ANY),
        scratch_shapes=[
            pltpu.VMEM((2, tm, HX_W), F32),
            pltpu.VMEM((2, tm, d), F32),
            pltpu.SemaphoreType.DMA((2,)),
            pltpu.SemaphoreType.DMA((2,)),
        ],
    )
    return pl.pallas_call(
        functools.partial(_moe_kernel, tm=tm, t_rows=t),
        grid_spec=grid_spec,
        out_shape=jax.ShapeDtypeStruct((t + 2 * tm, d), F32),
        compiler_params=_cparams(("arbitrary",)),
        name="moe",
    )(e_lo, e_hi, n_used, src, hx, wg, wu, wd, wg, wu, wd)


def _ple_kernel(h_ref, y_ref, p_ref, wg_ref, wu_ref, g_ref, b_ref, o_ref):
    h = h_ref[...]
    gate = jnp.dot(h.astype(BF16), wg_ref[...], preferred_element_type=F32)
    up = jnp.dot(p_ref[...].astype(BF16), wu_ref[...], preferred_element_type=F32)
    ple = up / (1.0 + jnp.exp(-gate))
    o_ref[...] = _layer_norm(ALPHA * h + y_ref[...] + ple, g_ref[...], b_ref[...])


def _ple_ln(hx, y, p2d, wg, wu, g, b, tm=256):
    t = hx.shape[0]
    d = D_MODEL
    row = lambda i: (i, 0)
    const2 = lambda i: (0, 0)
    once = pl.Buffered(1)
    return pl.pallas_call(
        _ple_kernel,
        grid=(t // tm,),
        in_specs=[
            pl.BlockSpec((tm, d), row),
            pl.BlockSpec((tm, d), row),
            pl.BlockSpec((tm, PLE_DIM), row),
            pl.BlockSpec((d, d), const2, pipeline_mode=once),
            pl.BlockSpec((PLE_DIM, d), const2, pipeline_mode=once),
            pl.BlockSpec((1, d), const2),
            pl.BlockSpec((1, d), const2),
        ],
        out_specs=pl.BlockSpec((tm, d), row),
        out_shape=jax.ShapeDtypeStruct((t, d), F32),
        compiler_params=_cparams(("parallel",)),
        name="ple_ln",
    )(hx, y, p2d, wg, wu, g, b)


def _t5_bucket(n):
    max_exact = REL_BUCKETS // 2
    nf = jnp.maximum(n, 1).astype(F32)
    large = max_exact + (jnp.log(nf / max_exact) / math.log(REL_MAX_DIST / max_exact)
                         * (REL_BUCKETS - max_exact)).astype(jnp.int32)
    large = jnp.minimum(large, REL_BUCKETS - 1)
    return jnp.where(n < max_exact, n, large)


def _pad_lanes(v, width=LANES):
    v = v.astype(F32).reshape(1, -1)
    return jnp.pad(v, ((0, 0), (0, width - v.shape[1])))


def _reorder_w_in(w):
    o_ff = 3 * GROUP_W
    o_d = o_ff + FOX_HEADS
    o_sk = o_d + 5 * GROUP_W
    o_sv = o_sk + SWA_KV_HEADS * HEAD_DIM
    sk = [w[:, o_sk + g * HEAD_DIM:o_sk + (g + 1) * HEAD_DIM] for g in range(SWA_KV_HEADS)]
    sv = [w[:, o_sv + g * HEAD_DIM:o_sv + (g + 1) * HEAD_DIM] for g in range(SWA_KV_HEADS)]
    main = jnp.concatenate(
        [w[:, :o_ff], w[:, o_d:o_sk]] + [m for g in sk for m in (g, g)] + [m for g in sv for m in (g, g)],
        axis=1)
    w_ff = jnp.pad(w[:, o_ff:o_d], ((0, 0), (0, LANES - FOX_HEADS)))
    return main.astype(BF16), w_ff.astype(BF16)


def kernel(x, p, rel_table, w_in, b_f, lam_q1, lam_k1, lam_q2, lam_k2, diff_norm_g, pool_w, pool_scale, sinks, w_o, ln1_g, ln1_b, router_g_w, router_g_b, router_e_w, router_e_b, w_gate, w_up, w_down, ple_gate_w, ple_up_w, ln2_g, ln2_b):
    b, s, d = x.shape
    t = b * s
    depth = w_in.shape[0]
    assert d == D_MODEL and s % (2 * ATT_TILE) == 0 and t % 1024 == 0

    bias_by_dist = rel_table[_t5_bucket(jnp.arange(s, dtype=jnp.int32))]
    diff_tiles = _diff_bias_tiles(bias_by_dist[:, :DIFF_HEADS], ATT_TILE)
    swa_tiles = _swa_bias_tiles(bias_by_dist[:SWA_WINDOW, DIFF_HEADS:])

    x2d = x.reshape(t, d)
    for i in range(depth):
        w_main, w_ff = _reorder_w_in(w_in[i])
        proj, ff = _inproj(x2d, w_main, w_ff)
        proj3 = proj.reshape(b, s, PROJ_W)

        c = _gate_cumsum(ff.reshape(b, s, LANES), _pad_lanes(b_f[i]))
        fqa, fka, fva, sqa, ska, sva = _attention_prep(proj3, c)
        y_fox = _fox_attention(fqa, fka, fva)

        lam_init = 0.8 - 0.6 * math.exp(-0.3 * i)
        lam_vecs = jnp.concatenate(
            [_pad_lanes(v) for v in (lam_q1[i], lam_k1[i], lam_q2[i], lam_k2[i])], axis=0)
        y_diff = _diff_attention(proj3, diff_tiles, lam_vecs, _pad_lanes(diff_norm_g[i]), lam_init)

        y_pool = _pool_mixer(proj3, pool_w[i].astype(BF16), pool_scale[i].reshape(1, GROUP_W))
        sink_rows = jnp.broadcast_to(
            sinks[i].astype(F32).reshape(SWA_KV_HEADS, SWA_GROUP, 1, 1),
            (SWA_KV_HEADS, SWA_GROUP, SWA_WINDOW, LANES)).reshape(SWA_KV_HEADS, -1, LANES)
        y_swa = _swa_attention(sqa, ska, sva, swa_tiles, sink_rows)

        w_r = jnp.concatenate([router_g_w[i], router_e_w[i]], axis=1)
        w_r = jnp.pad(w_r, ((0, 0), (0, LANES - w_r.shape[1]))).astype(BF16)
        r_b = _pad_lanes(jnp.concatenate([router_g_b[i], router_e_b[i]]))
        ys = [y.reshape(t, GROUP_W) for y in (y_fox, y_diff, y_pool, y_swa)]
        hx = _outproj_ln(
            ys, x2d, w_o[i].astype(BF16).reshape(4, GROUP_W, d),
            ln1_g[i].reshape(1, d), ln1_b[i].reshape(1, d), w_r, r_b)

        bucket = hx[:, D_MODEL + ROUTE_BUCKET].astype(jnp.int32)
        y_moe = _moe(hx, bucket, w_gate[i].astype(BF16), w_up[i].astype(BF16),
                     w_down[i].astype(BF16))
        x2d = _ple_ln(hx, y_moe, p[i].reshape(t, PLE_DIM), ple_gate_w[i].astype(BF16),
                      ple_up_w[i].astype(BF16), ln2_g[i].reshape(1, d), ln2_b[i].reshape(1, d))
    return x2d.reshape(b, s, d)
```

```python
import functools
import math

import jax
import jax.numpy as jnp
from jax import lax
from jax.experimental import pallas as pl
from jax.experimental.pallas import tpu as pltpu

F32 = jnp.float32
BF16 = jnp.bfloat16

D_MODEL = 2048
HEAD_DIM = 64
GROUP_W = D_MODEL // 4
FOX_HEADS = GROUP_W // HEAD_DIM
DIFF_HEADS = GROUP_W // (2 * HEAD_DIM)
POOL_GROUPS = 4
POOL_CH = GROUP_W // POOL_GROUPS
POOL_WINDOWS = (2, 4, 8, 16)
SWA_HEADS = GROUP_W // HEAD_DIM
SWA_KV_HEADS = 2
SWA_GROUP = SWA_HEADS // SWA_KV_HEADS
SWA_WINDOW = 128
REL_BUCKETS = 32
REL_MAX_DIST = 128
MOE_GROUPS = 4
EXPERTS_PER_GROUP = 4
N_EXPERTS = MOE_GROUPS * EXPERTS_PER_GROUP
D_EXPERT = D_MODEL // 4
PLE_DIM = 256
DEPTH = 2
ALPHA = (2 * DEPTH) ** 0.25
LN_EPS = 1e-5

LANES = 128
VMEM_LIMIT = 56 * 1024 * 1024

PROJ_W = 9 * GROUP_W
COL_FQ, COL_FK, COL_FV, COL_DQ, COL_DK, COL_DV, COL_PU, COL_SQ, COL_SKV = range(9)

ATT_TILE = 256
NEG = -1e30

Q_C = HEAD_DIM
K_C = HEAD_DIM + 3


def _cparams(sem, vmem=VMEM_LIMIT):
    return pltpu.CompilerParams(dimension_semantics=sem, vmem_limit_bytes=vmem)


def _inproj_kernel(x_ref, w_ref, wff_ref, o_ref, ff_ref, xb_ref):
    j = pl.program_id(1)

    @pl.when(j == 0)
    def _():
        xb_ref[...] = x_ref[...].astype(BF16)
        ff_ref[...] = jnp.dot(xb_ref[...], wff_ref[...], preferred_element_type=F32)

    o_ref[...] = jnp.dot(xb_ref[...], w_ref[...], preferred_element_type=F32).astype(BF16)


def _inproj(x2d, w_main, w_ff, layer, tm=1024, tn=GROUP_W):
    t, d = x2d.shape
    n = w_main.shape[2]
    tm = min(tm, t)
    return pl.pallas_call(
        _inproj_kernel,
        grid=(t // tm, n // tn),
        in_specs=[
            pl.BlockSpec((tm, d), lambda i, j: (i, 0)),
            pl.BlockSpec((None, d, tn), lambda i, j: (layer, 0, j)),
            pl.BlockSpec((None, d, LANES), lambda i, j: (layer, 0, 0)),
        ],
        out_specs=[
            pl.BlockSpec((tm, tn), lambda i, j: (i, j)),
            pl.BlockSpec((tm, LANES), lambda i, j: (i, 0)),
        ],
        out_shape=[
            jax.ShapeDtypeStruct((t, n), BF16),
            jax.ShapeDtypeStruct((t, LANES), F32),
        ],
        scratch_shapes=[pltpu.VMEM((tm, d), BF16)],
        compiler_params=_cparams(("parallel", "arbitrary")),
        name="inproj",
    )(x2d, w_main, w_ff)


def _gate_kernel(ff_ref, bf_ref, c_ref):
    f = ff_ref[0] + bf_ref[...]
    ls = jnp.minimum(f, 0.0) - jnp.log(1.0 + jnp.exp(-jnp.abs(f)))
    c = ls.T
    s_len = c.shape[1]
    lane = lax.broadcasted_iota(jnp.int32, c.shape, 1)
    shift = 1
    while shift < s_len:
        c = c + jnp.where(lane >= shift, pltpu.roll(c, shift, axis=1), 0.0)
        shift *= 2
    c_ref[0] = c.T


def _gate_cumsum(ff3, bf_pad):
    b, s, _ = ff3.shape
    return pl.pallas_call(
        _gate_kernel,
        grid=(b,),
        in_specs=[
            pl.BlockSpec((1, s, LANES), lambda i: (i, 0, 0)),
            pl.BlockSpec((1, LANES), lambda i: (0, 0)),
        ],
        out_specs=pl.BlockSpec((1, s, LANES), lambda i: (i, 0, 0)),
        out_shape=jax.ShapeDtypeStruct((b, s, LANES), F32),
        compiler_params=_cparams(("parallel",)),
        name="gate_cumsum",
    )(ff3, bf_pad)


def _lane_iota():
    return lax.broadcasted_iota(jnp.int32, (1, LANES), 1)


def _split3(c):
    hi = c.astype(BF16).astype(F32)
    r = c - hi
    lo = r.astype(BF16).astype(F32)
    lo2 = (r - lo).astype(BF16).astype(F32)
    return hi, lo, lo2


def _prep_kernel(fq_ref, fk_ref, fv_ref, sq_ref, skv_ref, c_ref,
                 fqa_ref, fka_ref, fva_ref, sqa_ref, ska_ref, sva_ref):
    scale = HEAD_DIM ** -0.5
    lane = _lane_iota()
    low = lane < HEAD_DIM
    ones_q = jnp.where((lane >= K_C) & (lane < K_C + 3), 1.0, 0.0)
    ones_k = jnp.where((lane >= Q_C) & (lane < Q_C + 3), 1.0, 0.0)
    parts = _split3(c_ref[0])

    def head(ref, h):
        x = ref[0, :, (h // 2) * LANES:(h // 2 + 1) * LANES].astype(F32)
        return pltpu.roll(x, HEAD_DIM, axis=1) if h % 2 else x

    def spread(h, first_lane):
        out = jnp.where(lane == first_lane, parts[0][:, h:h + 1], 0.0)
        for j in (1, 2):
            out = out + jnp.where(lane == first_lane + j, parts[j][:, h:h + 1], 0.0)
        return out

    rows = sqa_ref.shape[3]
    for h in range(FOX_HEADS):
        cols = slice(h * LANES, (h + 1) * LANES)
        q = jnp.where(low, head(fq_ref, h) * scale, 0.0) + spread(h, Q_C) + ones_q
        k = jnp.where(low, head(fk_ref, h), 0.0) - spread(h, K_C) + ones_k
        fqa_ref[0, :, cols] = q.astype(BF16)
        fka_ref[0, :, cols] = k.astype(BF16)
        fva_ref[0, :, cols] = jnp.where(low, head(fv_ref, h), 1.0).astype(BF16)
        sq = jnp.where(low, head(sq_ref, h) * scale, 0.0).astype(BF16)
        for j in range(sqa_ref.shape[1]):
            sqa_ref[0, j, h] = sq[j * rows:(j + 1) * rows]
    for g in range(SWA_KV_HEADS):
        cols = slice(g * LANES, (g + 1) * LANES)
        kv_w = SWA_KV_HEADS * LANES
        ska_ref[0, :, cols] = jnp.where(low, skv_ref[0, :, cols], jnp.zeros((), BF16))
        sva_ref[0, :, cols] = jnp.where(low, skv_ref[0, :, kv_w + g * LANES:kv_w + (g + 1) * LANES],
                                        jnp.ones((), BF16))


def _attention_prep(proj3, c, tp=ATT_TILE):
    b, s, _ = proj3.shape
    w = SWA_WINDOW
    kv_w = SWA_KV_HEADS * LANES
    col = lambda blk: (lambda i, j: (i, j, blk))
    wide = FOX_HEADS * LANES
    return pl.pallas_call(
        _prep_kernel,
        grid=(b, s // tp),
        in_specs=[
            pl.BlockSpec((1, tp, GROUP_W), col(COL_FQ)),
            pl.BlockSpec((1, tp, GROUP_W), col(COL_FK)),
            pl.BlockSpec((1, tp, GROUP_W), col(COL_FV)),
            pl.BlockSpec((1, tp, GROUP_W), col(COL_SQ)),
            pl.BlockSpec((1, tp, GROUP_W), col(COL_SKV)),
            pl.BlockSpec((1, tp, LANES), col(0)),
        ],
        out_specs=[
            pl.BlockSpec((1, tp, wide), col(0)),
            pl.BlockSpec((1, tp, wide), col(0)),
            pl.BlockSpec((1, tp, wide), col(0)),
            pl.BlockSpec((1, tp // w, SWA_HEADS, w, LANES), lambda i, j: (i, j, 0, 0, 0)),
            pl.BlockSpec((1, tp, kv_w), col(0)),
            pl.BlockSpec((1, tp, kv_w), col(0)),
        ],
        out_shape=[
            jax.ShapeDtypeStruct((b, s, wide), BF16),
            jax.ShapeDtypeStruct((b, s, wide), BF16),
            jax.ShapeDtypeStruct((b, s, wide), BF16),
            jax.ShapeDtypeStruct((b, s // w, SWA_HEADS, w, LANES), BF16),
            jax.ShapeDtypeStruct((b, s, kv_w), BF16),
            jax.ShapeDtypeStruct((b, s, kv_w), BF16),
        ],
        compiler_params=_cparams(("parallel", "parallel")),
        name="attention_prep",
    )(proj3, proj3, proj3, proj3, proj3, c)


def _causal_pairs(n_tiles):
    qi, ki = [], []
    for q in range(n_tiles):
        for k in range(q + 1):
            qi.append(q)
            ki.append(k)
    return jnp.asarray(qi, jnp.int32), jnp.asarray(ki, jnp.int32)


def _half_mask(half):
    lane = _lane_iota()
    return (lane >= HEAD_DIM) if half else (lane < HEAD_DIM)


def _qk(q, k):
    return lax.dot_general(q, k, (((1,), (1,)), ((), ())), preferred_element_type=F32)


def _softmax_step(s, m_prev):
    m_new = jnp.maximum(m_prev, jnp.max(s, axis=1, keepdims=True))
    alpha = jnp.exp(m_prev - m_new)
    p = jnp.concatenate(
        [jnp.exp(s[:, j * LANES:(j + 1) * LANES] - m_new) for j in range(s.shape[1] // LANES)],
        axis=1)
    return m_new, alpha, p.astype(BF16)


def _divide_by_upper_lanes(acc):
    return acc / pltpu.roll(acc, HEAD_DIM, axis=1)


def _join_heads(o_even, o_odd):
    return jnp.where(_half_mask(0), o_even, pltpu.roll(o_odd, HEAD_DIM, axis=1))


def _fox_kernel(qi_ref, ki_ref, q_ref, k_ref, v_ref, mask_ref, o_ref, m_sc, acc_sc):
    p_id = pl.program_id(1)
    qi = qi_ref[p_id]
    ki = ki_ref[p_id]

    @pl.when(ki == 0)
    def _():
        m_sc[...] = jnp.full(m_sc.shape, NEG, F32)
        acc_sc[...] = jnp.zeros(acc_sc.shape, F32)

    def attend(masked):
        for h in range(FOX_HEADS):
            cols = slice(h * LANES, (h + 1) * LANES)
            s = _qk(q_ref[0, :, cols], k_ref[0, :, cols])
            if masked:
                s = s + mask_ref[...]
            m_new, alpha, p = _softmax_step(s, m_sc[h])
            m_sc[h] = m_new
            acc_sc[h] = alpha * acc_sc[h] + jnp.dot(p, v_ref[0, :, cols],
                                                    preferred_element_type=F32)

    @pl.when(ki < qi)
    def _():
        attend(False)

    @pl.when(ki == qi)
    def _():
        attend(True)

    @pl.when(ki == qi)
    def _():
        for hp in range(FOX_HEADS // 2):
            o = _join_heads(_divide_by_upper_lanes(acc_sc[2 * hp]),
                            _divide_by_upper_lanes(acc_sc[2 * hp + 1]))
            o_ref[0, :, hp * LANES:(hp + 1) * LANES] = o.astype(BF16)


def _causal_mask_tile(t):
    r = lax.broadcasted_iota(jnp.int32, (t, t), 0)
    c = lax.broadcasted_iota(jnp.int32, (t, t), 1)
    return jnp.where(c <= r, 0.0, NEG).astype(F32)


def _fox_attention(fqa, fka, fva, t=ATT_TILE):
    b, s, wide = fqa.shape
    qi, ki = _causal_pairs(s // t)
    grid_spec = pltpu.PrefetchScalarGridSpec(
        num_scalar_prefetch=2,
        grid=(b, qi.shape[0]),
        in_specs=[
            pl.BlockSpec((1, t, wide), lambda i, p, qi, ki: (i, qi[p], 0)),
            pl.BlockSpec((1, t, wide), lambda i, p, qi, ki: (i, ki[p], 0)),
            pl.BlockSpec((1, t, wide), lambda i, p, qi, ki: (i, ki[p], 0)),
            pl.BlockSpec((t, t), lambda i, p, qi, ki: (0, 0)),
        ],
        out_specs=pl.BlockSpec((1, t, GROUP_W), lambda i, p, qi, ki: (i, qi[p], 0)),
        scratch_shapes=[
            pltpu.VMEM((FOX_HEADS, t, LANES), F32),
            pltpu.VMEM((FOX_HEADS, t, LANES), F32),
        ],
    )
    return pl.pallas_call(
        _fox_kernel,
        grid_spec=grid_spec,
        out_shape=jax.ShapeDtypeStruct((b, s, GROUP_W), BF16),
        compiler_params=_cparams(("parallel", "arbitrary")),
        name="fox_attention",
    )(qi, ki, fqa, fka, fva, _causal_mask_tile(t))


def _diff_kernel(qi_ref, ki_ref, q_ref, k_ref, v_ref, bias_ref, lam_ref, g_ref,
                 o_ref, m_sc, acc_sc, *, lam_init):
    p_id = pl.program_id(1)
    qi = qi_ref[p_id]
    ki = ki_ref[p_id]
    scale = HEAD_DIM ** -0.5

    @pl.when(ki == 0)
    def _():
        m_sc[...] = jnp.full(m_sc.shape, NEG, F32)
        acc_sc[...] = jnp.zeros(acc_sc.shape, F32)

    def attend(near):
        for h in range(DIFF_HEADS):
            cols = slice(h * LANES, (h + 1) * LANES)
            q2 = q_ref[0, :, cols]
            k2 = k_ref[0, :, cols]
            v2 = v_ref[0, :, cols]
            v_ones = jnp.concatenate([v2, jnp.ones_like(v2)], axis=1)
            for mm in range(2):
                idx = 2 * h + mm
                qm = jnp.where(_half_mask(mm), q2, jnp.zeros_like(q2)) * scale
                s = _qk(qm, k2)
                if near:
                    s = s + bias_ref[h, qi - ki]
                m_new, alpha, p = _softmax_step(s, m_sc[idx])
                m_sc[idx] = m_new
                pv = jnp.dot(p, v_ones, preferred_element_type=F32)
                acc_sc[idx] = jnp.concatenate([alpha, alpha], axis=1) * acc_sc[idx] + pv

    @pl.when(qi - ki >= 2)
    def _():
        attend(False)

    @pl.when(qi - ki == 1)
    def _():
        attend(True)

    @pl.when(ki == qi)
    def _():
        attend(True)

    @pl.when(ki == qi)
    def _():
        lv = lam_ref[...]
        lam = (jnp.exp(jnp.sum(lv[0:1] * lv[1:2], axis=1, keepdims=True))
               - jnp.exp(jnp.sum(lv[2:3] * lv[3:4], axis=1, keepdims=True)) + lam_init)
        for h in range(DIFF_HEADS):
            a1 = acc_sc[2 * h]
            a2 = acc_sc[2 * h + 1]
            o = a1[:, :LANES] / a1[:, LANES:] - lam * (a2[:, :LANES] / a2[:, LANES:])
            ms = jnp.mean(o * o, axis=1, keepdims=True)
            o = o * lax.rsqrt(ms + LN_EPS) * g_ref[...] * (1.0 - lam_init)
            o_ref[0, :, h * LANES:(h + 1) * LANES] = o.astype(BF16)


def _toeplitz(u, t):
    x = jnp.tile(u, (1, t + 1))[:, :2 * t * t].reshape(u.shape[0], t, 2 * t)
    return x[:, :, :t][:, :, ::-1]


def _diff_bias_tiles(diff_bias, t):
    assert t + 1 >= REL_MAX_DIST
    bt = diff_bias.T.astype(F32)
    bt = bt - bt[:, -1:]
    masked = jnp.full((bt.shape[0], t - 1), NEG, F32)
    diag = jnp.concatenate([masked, bt[:, :t]], axis=1)
    nxt = bt[:, 1:2 * t]
    return jnp.stack([_toeplitz(diag, t), _toeplitz(nxt, t)], axis=1)


def _diff_attention(proj3, bias_tiles, lam_vecs, g_pad, lam_init, t=ATT_TILE):
    b, s, _ = proj3.shape
    qi, ki = _causal_pairs(s // t)
    grid_spec = pltpu.PrefetchScalarGridSpec(
        num_scalar_prefetch=2,
        grid=(b, qi.shape[0]),
        in_specs=[
            pl.BlockSpec((1, t, GROUP_W), lambda i, p, qi, ki: (i, qi[p], COL_DQ)),
            pl.BlockSpec((1, t, GROUP_W), lambda i, p, qi, ki: (i, ki[p], COL_DK)),
            pl.BlockSpec((1, t, GROUP_W), lambda i, p, qi, ki: (i, ki[p], COL_DV)),
            pl.BlockSpec((DIFF_HEADS, 2, t, t), lambda i, p, qi, ki: (0, 0, 0, 0)),
            pl.BlockSpec((4, LANES), lambda i, p, qi, ki: (0, 0)),
            pl.BlockSpec((1, LANES), lambda i, p, qi, ki: (0, 0)),
        ],
        out_specs=pl.BlockSpec((1, t, GROUP_W), lambda i, p, qi, ki: (i, qi[p], 0)),
        scratch_shapes=[
            pltpu.VMEM((2 * DIFF_HEADS, t, LANES), F32),
            pltpu.VMEM((2 * DIFF_HEADS, t, 2 * LANES), F32),
        ],
    )
    return pl.pallas_call(
        functools.partial(_diff_kernel, lam_init=lam_init),
        grid_spec=grid_spec,
        out_shape=jax.ShapeDtypeStruct((b, s, GROUP_W), BF16),
        compiler_params=_cparams(("parallel", "arbitrary")),
        name="diff_attention",
    )(qi, ki, proj3, proj3, proj3, bias_tiles, lam_vecs, g_pad)


def _swa_kernel(q_ref, kc_ref, kp_ref, vc_ref, vp_ref, bias_ref, sink_ref, o_ref):
    n = pl.program_id(1)
    first = (n == 0).astype(jnp.int32)
    w = SWA_WINDOW
    low = _half_mask(0)
    for g in range(SWA_KV_HEADS):
        kv = slice(g * LANES, (g + 1) * LANES)
        q = q_ref[0, 0, g * SWA_GROUP:(g + 1) * SWA_GROUP].reshape(SWA_GROUP * w, LANES)
        k = jnp.concatenate([kp_ref[0, :, kv], kc_ref[0, :, kv]], axis=0)
        v = jnp.concatenate([vp_ref[0, :, kv], vc_ref[0, :, kv]], axis=0)
        s = _qk(q, k) + bias_ref[first, g]
        sink = sink_ref[g]
        m, _, e = _softmax_step(s, sink)
        tot = jnp.dot(e, v, preferred_element_type=F32) + jnp.where(low, 0.0, jnp.exp(sink - m))
        o = _divide_by_upper_lanes(tot)
        for j in range(SWA_GROUP // 2):
            pair = g * (SWA_GROUP // 2) + j
            o_ref[0, :, pair * LANES:(pair + 1) * LANES] = _join_heads(
                o[2 * j * w:(2 * j + 1) * w], o[(2 * j + 1) * w:(2 * j + 2) * w]).astype(BF16)


def _swa_bias_tiles(swa_bias):
    w = SWA_WINDOW
    bt = swa_bias.T.astype(F32)
    heads = bt.shape[0]
    prev = jnp.concatenate([bt[:, 1:], jnp.full((heads, w), NEG, F32)], axis=1)
    cur = jnp.concatenate([jnp.full((heads, w - 1), NEG, F32), bt], axis=1)
    prev_t = _toeplitz(prev, w)
    cur_t = _toeplitz(cur, w)
    sets = [jnp.concatenate([pt, cur_t], axis=2) for pt in (prev_t, jnp.full_like(prev_t, NEG))]
    return jnp.stack(sets).reshape(2, SWA_KV_HEADS, SWA_GROUP * w, 2 * w)


def _swa_attention(sqa, ska, sva, bias_tiles, sink_rows):
    b, nb, _, w, _ = sqa.shape
    kv_w = SWA_KV_HEADS * LANES
    cur = lambda i, n: (i, n, 0)
    prev = lambda i, n: (i, jnp.maximum(n - 1, 0), 0)
    return pl.pallas_call(
        _swa_kernel,
        grid=(b, nb),
        in_specs=[
            pl.BlockSpec((1, 1, SWA_HEADS, w, LANES), lambda i, n: (i, n, 0, 0, 0)),
            pl.BlockSpec((1, w, kv_w), cur),
            pl.BlockSpec((1, w, kv_w), prev),
            pl.BlockSpec((1, w, kv_w), cur),
            pl.BlockSpec((1, w, kv_w), prev),
            pl.BlockSpec((2, SWA_KV_HEADS, SWA_GROUP * w, 2 * w), lambda i, n: (0, 0, 0, 0)),
            pl.BlockSpec((SWA_KV_HEADS, SWA_GROUP * w, LANES), lambda i, n: (0, 0, 0)),
        ],
        out_specs=pl.BlockSpec((1, w, GROUP_W), cur),
        out_shape=jax.ShapeDtypeStruct((b, nb * w, GROUP_W), BF16),
        compiler_params=_cparams(("parallel", "arbitrary")),
        name="swa_attention",
    )(sqa, ska, ska, sva, sva, bias_tiles, sink_rows)


def _pool_kernel(u_ref, w_ref, sc_ref, o_ref):
    s_len = u_ref.shape[1]
    row = lax.broadcasted_iota(jnp.int32, (s_len, POOL_CH), 0)
    for g in range(POOL_GROUPS):
        cols = slice(g * POOL_CH, (g + 1) * POOL_CH)
        u = u_ref[0, :, cols].astype(F32)
        win = u
        shift = 1
        while shift < POOL_WINDOWS[g]:
            win = win + jnp.where(row >= shift, pltpu.roll(win, shift, axis=0), 0.0)
            shift *= 2
        cnt = jnp.minimum(row + 1, POOL_WINDOWS[g]).astype(F32)
        pooled = (win / cnt - u).astype(BF16)
        y = jnp.dot(pooled, w_ref[g], preferred_element_type=F32) * sc_ref[:, cols]
        o_ref[0, :, cols] = y.astype(BF16)


def _pool_mixer(proj3, pool_w_bf, pool_scale_row, layer):
    b, s, _ = proj3.shape
    return pl.pallas_call(
        _pool_kernel,
        grid=(b,),
        in_specs=[
            pl.BlockSpec((1, s, GROUP_W), lambda i: (i, 0, COL_PU)),
            pl.BlockSpec((None, POOL_GROUPS, POOL_CH, POOL_CH), lambda i: (layer, 0, 0, 0)),
            pl.BlockSpec((1, GROUP_W), lambda i: (0, 0)),
        ],
        out_specs=pl.BlockSpec((1, s, GROUP_W), lambda i: (i, 0, 0)),
        out_shape=jax.ShapeDtypeStruct((b, s, GROUP_W), BF16),
        compiler_params=_cparams(("parallel",)),
        name="pool_mixer",
    )(proj3, pool_w_bf, pool_scale_row)


def _layer_norm(z, g, b):
    mu = jnp.mean(z, axis=1, keepdims=True)
    zc = z - mu
    var = jnp.mean(zc * zc, axis=1, keepdims=True)
    return zc * lax.rsqrt(var + LN_EPS) * g + b


PAIRS = [(i, j) for i in range(EXPERTS_PER_GROUP) for j in range(i + 1, EXPERTS_PER_GROUP)]
N_BUCKETS = MOE_GROUPS * len(PAIRS)
ROUTE_BUCKET, ROUTE_W_LO, ROUTE_W_HI = 0, 1, 2
HX_W = D_MODEL + LANES


def _first_max(vals):
    best = vals[0]
    for v in vals[1:]:
        best = jnp.maximum(best, v)
    idx = jnp.full(best.shape, len(vals) - 1, jnp.int32)
    for j in range(len(vals) - 2, -1, -1):
        idx = jnp.where(vals[j] == best, j, idx)
    return best, idx


def _route_block(rl):
    lt = rl.T
    gl = [lt[g:g + 1] for g in range(MOE_GROUPS)]
    gmax, gsel = _first_max(gl)
    gsum = gl[0] * 0.0
    for v in gl:
        gsum = gsum + jnp.exp(v - gmax)
    g_p = 1.0 / gsum
    e_in = []
    for j in range(EXPERTS_PER_GROUP):
        v = lt[MOE_GROUPS + j:MOE_GROUPS + j + 1]
        for g in range(1, MOE_GROUPS):
            row = MOE_GROUPS + g * EXPERTS_PER_GROUP + j
            v = jnp.where(gsel == g, lt[row:row + 1], v)
        e_in.append(v)
    v1, i1 = _first_max(e_in)
    rest = [jnp.where(i1 == j, NEG, e_in[j]) for j in range(EXPERTS_PER_GROUP)]
    v2, i2 = _first_max(rest)
    e21 = jnp.exp(v2 - v1)
    w1 = g_p / (1.0 + e21)
    w2 = g_p * e21 / (1.0 + e21)
    lo = jnp.minimum(i1, i2)
    hi = jnp.maximum(i1, i2)
    pair = jnp.zeros_like(lo)
    for n, (a, b) in enumerate(PAIRS):
        pair = jnp.where((lo == a) & (hi == b), n, pair)
    bucket = (gsel * len(PAIRS) + pair).astype(F32)
    w_lo = jnp.where(i1 < i2, w1, w2)
    w_hi = jnp.where(i1 < i2, w2, w1)
    sub = lax.broadcasted_iota(jnp.int32, lt.shape, 0)
    out = jnp.where(sub == ROUTE_BUCKET, bucket,
                    jnp.where(sub == ROUTE_W_LO, w_lo, jnp.where(sub == ROUTE_W_HI, w_hi, 0.0)))
    return out.T


def _outproj_kernel(yf_ref, yd_ref, yp_ref, ys_ref, x_ref, wo_ref, g_ref, b_ref, wr_ref, rb_ref,
                    hx_ref):
    acc = jnp.dot(yf_ref[...], wo_ref[0], preferred_element_type=F32)
    acc += jnp.dot(yd_ref[...], wo_ref[1], preferred_element_type=F32)
    acc += jnp.dot(yp_ref[...], wo_ref[2], preferred_element_type=F32)
    acc += jnp.dot(ys_ref[...], wo_ref[3], preferred_element_type=F32)
    h = _layer_norm(ALPHA * x_ref[...] + acc, g_ref[...], b_ref[...])
    hx_ref[:, :D_MODEL] = h
    rl = jnp.dot(h.astype(BF16), wr_ref[...], preferred_element_type=F32) + rb_ref[...]
    hx_ref[:, D_MODEL:] = _route_block(rl)


def _outproj_ln(ys, x2d, wo4, g, b, wr, rb, layer, tm=512):
    t, d = x2d.shape
    row = lambda i: (i, 0)
    const2 = lambda i: (0, 0)
    once = pl.Buffered(1)
    return pl.pallas_call(
        _outproj_kernel,
        grid=(t // tm,),
        in_specs=[pl.BlockSpec((tm, GROUP_W), row)] * 4 + [
            pl.BlockSpec((tm, d), row),
            pl.BlockSpec((None, 4, GROUP_W, d), lambda i: (layer, 0, 0, 0), pipeline_mode=once),
            pl.BlockSpec((1, d), const2),
            pl.BlockSpec((1, d), const2),
            pl.BlockSpec((d, LANES), const2, pipeline_mode=once),
            pl.BlockSpec((1, LANES), const2),
        ],
        out_specs=pl.BlockSpec((tm, HX_W), row),
        out_shape=jax.ShapeDtypeStruct((t, HX_W), F32),
        compiler_params=_cparams(("parallel",)),
        name="outproj_ln",
    )(*ys, x2d, wo4, g, b, wr, rb)


MOE_TILE = 256


def _route_plan(bucket, tm, n_tiles):
    t = bucket.shape[0]
    onehot = (bucket[:, None] == jnp.arange(N_BUCKETS, dtype=jnp.int32)[None, :]).astype(jnp.int32)
    csum = jnp.cumsum(onehot, axis=0)
    rank = jnp.sum(onehot * csum, axis=1) - 1
    tiles_b = (csum[-1] + tm - 1) // tm
    tile_end = jnp.cumsum(tiles_b)
    slot0 = (tile_end - tiles_b) * tm
    pos = jnp.sum(onehot * slot0[None, :], axis=1) + rank
    src = jnp.full((n_tiles * tm,), -1, jnp.int32).at[pos].set(jnp.arange(t, dtype=jnp.int32))
    n_used = tile_end[-1:]
    tile = jnp.minimum(jnp.arange(n_tiles, dtype=jnp.int32), n_used - 1)
    tile_bucket = jnp.sum((tile[:, None] >= tile_end[None, :]).astype(jnp.int32), axis=1)
    group = tile_bucket // len(PAIRS)
    pair = tile_bucket % len(PAIRS)
    pair_lo = jnp.asarray([a for a, _ in PAIRS], jnp.int32)
    pair_hi = jnp.asarray([b for _, b in PAIRS], jnp.int32)
    sel = (pair[:, None] == jnp.arange(len(PAIRS), dtype=jnp.int32)[None, :]).astype(jnp.int32)
    e_lo = group * EXPERTS_PER_GROUP + jnp.sum(sel * pair_lo[None, :], axis=1)
    e_hi = group * EXPERTS_PER_GROUP + jnp.sum(sel * pair_hi[None, :], axis=1)
    return e_lo, e_hi, n_used, src


def _moe_kernel(elo_ref, ehi_ref, nused_ref, src_ref,
                hx_hbm, wgl_ref, wul_ref, wdl_ref, wgh_ref, wuh_ref, wdh_ref,
                y_hbm, hbuf, obuf, gsem, ssem, *, tm, t_rows):
    i = pl.program_id(0)
    n_used = nused_ref[0]
    sub = hbuf.shape[2]

    def hbm_row(ref, row):
        return ref.at[lax.shift_right_logical(row, 3), pl.ds(jnp.bitwise_and(row, sub - 1), 1)]

    def start_gather(tile, sl):
        def body(k, carry):
            for j in range(sub):
                row = jnp.maximum(src_ref[tile * tm + k * sub + j], 0)
                pltpu.make_async_copy(hbm_row(hx_hbm, row), hbuf.at[sl, k, pl.ds(j, 1)],
                                      gsem.at[sl]).start()
            return carry
        lax.fori_loop(0, tm // sub, body, 0)

    def start_scatter(tile, sl):
        def body(k, carry):
            for j in range(sub):
                r = k * sub + j
                row = src_ref[tile * tm + r]
                row = jnp.where(row < 0, t_rows + sl * tm + r, row)
                pltpu.make_async_copy(obuf.at[sl, k, pl.ds(j, 1)], hbm_row(y_hbm, row),
                                      ssem.at[sl]).start(priority=1)
            return carry
        lax.fori_loop(0, tm // sub, body, 0)

    def wait_gather(sl):
        pltpu.make_async_copy(hx_hbm.at[pl.ds(0, tm // sub)], hbuf.at[sl], gsem.at[sl]).wait()

    def wait_scatter(sl):
        pltpu.make_async_copy(obuf.at[sl], y_hbm.at[pl.ds(0, tm // sub)], ssem.at[sl]).wait()

    @pl.when(i == 0)
    def _():
        start_gather(0, 0)
        obuf[...] = jnp.zeros(obuf.shape, F32)
        for sl in range(2):
            spare = pltpu.make_async_copy(
                obuf.at[sl], y_hbm.at[pl.ds((t_rows + sl * tm) // sub, tm // sub)], ssem.at[sl])
            spare.start()
            spare.wait()

    def step(sl):
        @pl.when(i + 1 < n_used)
        def _():
            start_gather(i + 1, 1 - sl)

        @pl.when(i < n_used)
        def _():
            wait_gather(sl)
            rows = hbuf[sl].reshape(tm, HX_W)
            x = rows[:, :D_MODEL].astype(BF16)
            y = None
            for wg_ref, wu_ref, wd_ref, lane in ((wgl_ref, wul_ref, wdl_ref, ROUTE_W_LO),
                                                 (wgh_ref, wuh_ref, wdh_ref, ROUTE_W_HI)):
                a = jnp.dot(x, wg_ref[0], preferred_element_type=F32)
                u = jnp.dot(x, wu_ref[0], preferred_element_type=F32)
                gate = rows[:, D_MODEL + lane:D_MODEL + lane + 1]
                hid = (a / (1.0 + jnp.exp(-a))) * u * gate
                part = jnp.dot(hid.astype(BF16), wd_ref[0], preferred_element_type=F32)
                y = part if y is None else y + part

            @pl.when(i >= 2)
            def _():
                wait_scatter(sl)

            obuf[sl] = y.reshape(tm // sub, sub, D_MODEL)
            start_scatter(i, sl)

    for sl in range(2):
        pl.when(lax.rem(i, 2) == sl)(functools.partial(step, sl))

    @pl.when(i == pl.num_programs(0) - 1)
    def _():
        wait_scatter(0)
        wait_scatter(1)


def _moe(hx, bucket, wg, wu, wd, layer, tm=MOE_TILE):
    t = hx.shape[0]
    d = D_MODEL
    f = wg.shape[3]
    sub = 8
    n_tiles = t // tm + N_BUCKETS
    e_lo, e_hi, n_used, src = _route_plan(bucket, tm, n_tiles)
    lo3 = lambda i, elo, ehi, nu, src: (layer, elo[i], 0, 0)
    hi3 = lambda i, elo, ehi, nu, src: (layer, ehi[i], 0, 0)
    grid_spec = pltpu.PrefetchScalarGridSpec(
        num_scalar_prefetch=4,
        grid=(n_tiles,),
        in_specs=[
            pl.BlockSpec(memory_space=pl.ANY),
            pl.BlockSpec((None, 1, d, f), lo3),
            pl.BlockSpec((None, 1, d, f), lo3),
            pl.BlockSpec((None, 1, f, d), lo3),
            pl.BlockSpec((None, 1, d, f), hi3),
            pl.BlockSpec((None, 1, d, f), hi3),
            pl.BlockSpec((None, 1, f, d), hi3),
        ],
        out_specs=pl.BlockSpec(memory_space=pl.ANY),
        scratch_shapes=[
            pltpu.VMEM((2, tm // sub, sub, HX_W), F32),
            pltpu.VMEM((2, tm // sub, sub, d), F32),
            pltpu.SemaphoreType.DMA((2,)),
            pltpu.SemaphoreType.DMA((2,)),
        ],
    )
    y = pl.pallas_call(
        functools.partial(_moe_kernel, tm=tm, t_rows=t),
        grid_spec=grid_spec,
        out_shape=jax.ShapeDtypeStruct(((t + 2 * tm) // sub, sub, d), F32),
        compiler_params=_cparams(("arbitrary",)),
        name="moe",
    )(e_lo, e_hi, n_used, src, hx.reshape(t // sub, sub, HX_W), wg, wu, wd, wg, wu, wd)
    return y.reshape(t + 2 * tm, d)


def _ple_kernel(h_ref, y_ref, p_ref, wg_ref, wu_ref, g_ref, b_ref, o_ref):
    h = h_ref[...]
    gate = jnp.dot(h.astype(BF16), wg_ref[...], preferred_element_type=F32)
    up = jnp.dot(p_ref[...].astype(BF16), wu_ref[...], preferred_element_type=F32)
    ple = up / (1.0 + jnp.exp(-gate))
    o_ref[...] = _layer_norm(ALPHA * h + y_ref[...] + ple, g_ref[...], b_ref[...])


def _ple_ln(hx, y, p3, wg, wu, g, b, layer, tm=512):
    t = hx.shape[0]
    d = D_MODEL
    row = lambda i: (i, 0)
    const2 = lambda i: (0, 0)
    once = pl.Buffered(1)
    return pl.pallas_call(
        _ple_kernel,
        grid=(t // tm,),
        in_specs=[
            pl.BlockSpec((tm, d), row),
            pl.BlockSpec((tm, d), row),
            pl.BlockSpec((None, tm, PLE_DIM), lambda i: (layer, i, 0)),
            pl.BlockSpec((None, d, d), lambda i: (layer, 0, 0), pipeline_mode=once),
            pl.BlockSpec((None, PLE_DIM, d), lambda i: (layer, 0, 0), pipeline_mode=once),
            pl.BlockSpec((1, d), const2),
            pl.BlockSpec((1, d), const2),
        ],
        out_specs=pl.BlockSpec((tm, d), row),
        out_shape=jax.ShapeDtypeStruct((t, d), F32),
        compiler_params=_cparams(("parallel",)),
        name="ple_ln",
    )(hx, y, p3, wg, wu, g, b)


def _t5_bucket(n):
    max_exact = REL_BUCKETS // 2
    nf = jnp.maximum(n, 1).astype(F32)
    large = max_exact + (jnp.log(nf / max_exact) / math.log(REL_MAX_DIST / max_exact)
                         * (REL_BUCKETS - max_exact)).astype(jnp.int32)
    large = jnp.minimum(large, REL_BUCKETS - 1)
    return jnp.where(n < max_exact, n, large)


def _pad_lanes(v, width=LANES):
    v = v.astype(F32).reshape(1, -1)
    return jnp.pad(v, ((0, 0), (0, width - v.shape[1])))


def _reorder_w_in(w):
    o_ff = 3 * GROUP_W
    o_d = o_ff + FOX_HEADS
    o_sk = o_d + 5 * GROUP_W
    o_sv = o_sk + SWA_KV_HEADS * HEAD_DIM
    sk = [w[..., o_sk + g * HEAD_DIM:o_sk + (g + 1) * HEAD_DIM] for g in range(SWA_KV_HEADS)]
    sv = [w[..., o_sv + g * HEAD_DIM:o_sv + (g + 1) * HEAD_DIM] for g in range(SWA_KV_HEADS)]
    main = jnp.concatenate(
        [w[..., :o_ff], w[..., o_d:o_sk]] + [m for g in sk for m in (g, g)]
        + [m for g in sv for m in (g, g)], axis=-1)
    w_ff = jnp.pad(w[..., o_ff:o_d], ((0, 0), (0, 0), (0, LANES - FOX_HEADS)))
    return main.astype(BF16), w_ff.astype(BF16)


def kernel(x, p, rel_table, w_in, b_f, lam_q1, lam_k1, lam_q2, lam_k2, diff_norm_g, pool_w, pool_scale, sinks, w_o, ln1_g, ln1_b, router_g_w, router_g_b, router_e_w, router_e_b, w_gate, w_up, w_down, ple_gate_w, ple_up_w, ln2_g, ln2_b):
    b, s, d = x.shape
    t = b * s
    depth = w_in.shape[0]
    assert d == D_MODEL and s % (2 * ATT_TILE) == 0 and t % 1024 == 0

    bias_by_dist = rel_table[_t5_bucket(jnp.arange(s, dtype=jnp.int32))]
    diff_tiles = _diff_bias_tiles(bias_by_dist[:, :DIFF_HEADS], ATT_TILE)
    swa_tiles = _swa_bias_tiles(bias_by_dist[:SWA_WINDOW, DIFF_HEADS:])

    w_main, w_ff = _reorder_w_in(w_in)
    wo_all = w_o.astype(BF16).reshape(depth, 4, GROUP_W, d)
    pool_w_all = pool_w.astype(BF16)
    wg_all, wu_all, wd_all = w_gate.astype(BF16), w_up.astype(BF16), w_down.astype(BF16)
    pg_all, pu_all = ple_gate_w.astype(BF16), ple_up_w.astype(BF16)
    p3 = p.reshape(depth, t, PLE_DIM)

    x2d = x.reshape(t, d)
    for i in range(depth):
        proj, ff = _inproj(x2d, w_main, w_ff, i)
        proj3 = proj.reshape(b, s, PROJ_W)

        c = _gate_cumsum(ff.reshape(b, s, LANES), _pad_lanes(b_f[i]))
        fqa, fka, fva, sqa, ska, sva = _attention_prep(proj3, c)
        y_fox = _fox_attention(fqa, fka, fva)

        lam_init = 0.8 - 0.6 * math.exp(-0.3 * i)
        lam_vecs = jnp.concatenate(
            [_pad_lanes(v) for v in (lam_q1[i], lam_k1[i], lam_q2[i], lam_k2[i])], axis=0)
        y_diff = _diff_attention(proj3, diff_tiles, lam_vecs, _pad_lanes(diff_norm_g[i]), lam_init)

        y_pool = _pool_mixer(proj3, pool_w_all, pool_scale[i].reshape(1, GROUP_W), i)
        sink_rows = jnp.broadcast_to(
            sinks[i].astype(F32).reshape(SWA_KV_HEADS, SWA_GROUP, 1, 1),
            (SWA_KV_HEADS, SWA_GROUP, SWA_WINDOW, LANES)).reshape(SWA_KV_HEADS, -1, LANES)
        y_swa = _swa_attention(sqa, ska, sva, swa_tiles, sink_rows)

        w_r = jnp.concatenate([router_g_w[i], router_e_w[i]], axis=1)
        w_r = jnp.pad(w_r, ((0, 0), (0, LANES - w_r.shape[1]))).astype(BF16)
        r_b = _pad_lanes(jnp.concatenate([router_g_b[i], router_e_b[i]]))
        ys = [y.reshape(t, GROUP_W) for y in (y_fox, y_diff, y_pool, y_swa)]
        hx = _outproj_ln(ys, x2d, wo_all, ln1_g[i].reshape(1, d), ln1_b[i].reshape(1, d), w_r, r_b, i)

        bucket = hx[:, D_MODEL + ROUTE_BUCKET].astype(jnp.int32)
        y_moe = _moe(hx, bucket, wg_all, wu_all, wd_all, i)
        x2d = _ple_ln(hx, y_moe, p3, pg_all, pu_all, ln2_g[i].reshape(1, d), ln2_b[i].reshape(1, d), i)
    return x2d.reshape(b, s, d)
```
